```python
import math
import jax
import jax.numpy as jnp
from jax import lax
import numpy as np

D_MODEL = 2048
BATCH = 1
SEQ = 8192
DEPTH = 4

GRID_W = 64
CTX_LEN = 256

F32 = jnp.float32
N_MOD = 6
RMS_EPS = 1e-6

DIFF_HEADS = D_MODEL // 256
DIFF_QK_DIM = 64
DIFF_V_DIM = 2 * DIFF_QK_DIM
DIFF_QK_W = 2 * DIFF_HEADS * DIFF_QK_DIM
DIFF_V_W = DIFF_HEADS * DIFF_V_DIM
Q_BLOCK = 128
ROPE_BASE = 10000.0
ROPE_AXIS_DIM = DIFF_QK_DIM // 2

S5_WIDTH = D_MODEL // 2
S5_P = 16
S5_GROUPS = S5_WIDTH // S5_P
S5_STATE = 64

EVEN_IN_W = 2 * DIFF_QK_W + DIFF_V_W + S5_WIDTH
EVEN_OUT_W = DIFF_V_W + S5_WIDTH

SSD_INNER = 2 * D_MODEL
SSD_HEAD_DIM = 64
SSD_HEADS = SSD_INNER // SSD_HEAD_DIM
SSD_GROUPS = 8
SSD_HPG = SSD_HEADS // SSD_GROUPS
SSD_STATE = 128
SSD_CONV = 5
SSD_CHUNK = 128
SSD_CONV_CH = SSD_INNER + 2 * SSD_GROUPS * SSD_STATE
ODD_IN_W = SSD_INNER + SSD_CONV_CH + 2 * SSD_HEADS

MLP_HIDDEN = 4 * D_MODEL

kernel_name = 'hybrid_diffattn_s5_ssd_prefix_dit'


def _rms(x, g):
    xf = x.astype(F32)
    y = xf * lax.rsqrt(jnp.mean(xf * xf, axis=-1, keepdims=True) + RMS_EPS) * g.astype(F32)
    return y.astype(x.dtype)


def _modulate(x, shift, scale):
    return x * (1 + scale) + shift


def _flip(t, on):
    return jnp.flip(t, axis=1) if on else t


def _axial_rope_tables(seq_len):
    rows = seq_len // GRID_W
    row = jnp.repeat(jnp.arange(rows, dtype=F32), GRID_W)
    col = jnp.tile(jnp.arange(GRID_W, dtype=F32), rows)
    inv = ROPE_BASE ** (-jnp.arange(0, ROPE_AXIS_DIM, 2, dtype=F32) / ROPE_AXIS_DIM)
    ang_r = row[:, None] * inv
    ang_c = col[:, None] * inv
    ang = jnp.concatenate([ang_r, ang_r, ang_c, ang_c], axis=-1)
    return jnp.cos(ang), jnp.sin(ang)


def _rope(x, cos, sin):
    xr1, xr2, xc1, xc2 = jnp.split(x, 4, axis=-1)
    rot = jnp.concatenate([-xr2, xr1, -xc2, xc1], axis=-1)
    cb = cos[None, :, None, None, :]
    sb = sin[None, :, None, None, :]
    return (x * cb + rot * sb).astype(x.dtype)


def _diff_attn(q, k, v, lam):
    s = jnp.einsum('bqmhd,bkmhd->bmhqk', q, k).astype(F32) * (DIFF_QK_DIM ** -0.5)
    p = jax.nn.softmax(s, axis=-1)
    w = p[:, 0] - lam * p[:, 1]
    return jnp.einsum('bhqk,bkhe->bqhe', w.astype(v.dtype), v)


def _diff_head_out(o, subln_g, lam_init):
    bsz, l = o.shape[:2]
    return (_rms(o, subln_g) * (1.0 - lam_init)).reshape(bsz, l, DIFF_V_W)


def _split_even(p):
    bsz, l = p.shape[:2]
    q = p[..., :DIFF_QK_W].reshape(bsz, l, 2, DIFF_HEADS, DIFF_QK_DIM)
    k = p[..., DIFF_QK_W:2 * DIFF_QK_W].reshape(bsz, l, 2, DIFF_HEADS, DIFF_QK_DIM)
    v = p[..., 2 * DIFF_QK_W:2 * DIFF_QK_W + DIFF_V_W].reshape(bsz, l, DIFF_HEADS, DIFF_V_DIM)
    s = p[..., 2 * DIFF_QK_W + DIFF_V_W:]
    return q, k, v, s


def _s5_discretize(lam_re, lam_im, log_dt, b_re, b_im):
    lam = lax.complex(lam_re.astype(F32), lam_im.astype(F32))
    dt = jnp.exp(log_dt.astype(F32))[:, None]
    lam_bar = jnp.exp(lam * dt)
    b = lax.complex(b_re.astype(F32), b_im.astype(F32))
    b_bar = ((lam_bar - 1) / lam)[..., None] * b
    return lam_bar, b_bar


def _s5_scan(bu, lam_bar, h0, reverse):
    if h0 is not None:
        bu = bu.at[:, -1 if reverse else 0].add(lam_bar * h0)
    a = jnp.broadcast_to(lam_bar, bu.shape)

    def combine(e1, e2):
        a1, b1 = e1
        a2, b2 = e2
        return a1 * a2, a2 * b1 + b2

    _, h = lax.associative_scan(combine, (a, bu), reverse=reverse, axis=1)
    return h


def _s5(us, usc, lam_re, lam_im, log_dt, b_re, b_im, c_re, c_im, d_skip, w_glu, b_glu, ctx_out):
    bsz, l = us.shape[:2]
    lc = usc.shape[1]
    u = us.astype(F32).reshape(bsz, l, S5_GROUPS, S5_P)
    uc = usc.astype(F32).reshape(bsz, lc, S5_GROUPS, S5_P)
    dsk = d_skip.astype(F32)
    y = u * dsk
    yc = uc * dsk
    for d, rev in enumerate((False, True)):
        lam_bar, b_bar = _s5_discretize(lam_re[d], lam_im[d], log_dt[d], b_re[d], b_im[d])
        c_mat = lax.complex(c_re[d].astype(F32), c_im[d].astype(F32))
        h_ctx = _s5_scan(jnp.einsum('bsgp,gnp->bsgn', uc.astype(jnp.complex64), b_bar), lam_bar, None, rev)
        h0 = h_ctx[:, 0] if rev else h_ctx[:, -1]
        h = _s5_scan(jnp.einsum('bsgp,gnp->bsgn', u.astype(jnp.complex64), b_bar), lam_bar, h0, rev)
        y = y + jnp.real(jnp.einsum('bsgn,gpn->bsgp', h, c_mat))
        if ctx_out:
            yc = yc + jnp.real(jnp.einsum('bsgn,gpn->bsgp', h_ctx, c_mat))

    def glu(yv, length):
        g = jax.nn.gelu(yv.reshape(bsz, length, S5_WIDTH))
        return (g * jax.nn.sigmoid(g @ w_glu.astype(F32) + b_glu.astype(F32))).astype(us.dtype)

    return glu(y, l), (glu(yc, lc) if ctx_out else None)


def _even_mixer(u, uc, w_in, w_out, lam_p, subln_g, s5_lam_re, s5_lam_im, s5_log_dt, s5_b_re, s5_b_im,
                s5_c_re, s5_c_im, s5_d, s5_w_glu, s5_b_glu, cos, sin, lam_init, ctx_out):
    bsz, l = u.shape[:2]
    q, k, v, us = _split_even(u @ w_in)
    qc, kc, vc, usc = _split_even(uc @ w_in)
    q = _rope(q, cos, sin)
    k = _rope(k, cos, sin)
    lp = lam_p.astype(F32)
    lam = jnp.exp(jnp.sum(lp[0] * lp[1])) - jnp.exp(jnp.sum(lp[2] * lp[3])) + lam_init
    k_all = jnp.concatenate([kc, k], axis=1)
    v_all = jnp.concatenate([vc, v], axis=1)
    nb = l // Q_BLOCK
    qb = jnp.moveaxis(q.reshape(bsz, nb, Q_BLOCK, 2, DIFF_HEADS, DIFF_QK_DIM), 1, 0)
    o = lax.map(lambda blk: _diff_attn(blk, k_all, v_all, lam), qb)
    o = jnp.moveaxis(o, 0, 1).reshape(bsz, l, DIFF_HEADS, DIFF_V_DIM)
    s, sc = _s5(us, usc, s5_lam_re, s5_lam_im, s5_log_dt, s5_b_re, s5_b_im, s5_c_re, s5_c_im,
                s5_d, s5_w_glu, s5_b_glu, ctx_out)
    y = jnp.concatenate([_diff_head_out(o, subln_g, lam_init), s], axis=-1) @ w_out
    yc = None
    if ctx_out:
        oc = _diff_attn(qc, kc, vc, lam)
        yc = jnp.concatenate([_diff_head_out(oc, subln_g, lam_init), sc], axis=-1) @ w_out
    return y, yc


def _dwconv(x, w, b):
    ch = x.shape[-1]
    y = lax.conv_general_dilated(x, w[:, None, :], window_strides=(1,),
                                 padding=[(SSD_CONV // 2, SSD_CONV // 2)],
                                 dimension_numbers=('NWC', 'WIO', 'NWC'),
                                 feature_group_count=ch)
    return y + b


def _segsum(a):
    t = a.shape[-1]
    cs = jnp.cumsum(a, axis=-1)
    diff = cs[..., :, None] - cs[..., None, :]
    return jnp.where(jnp.tril(jnp.ones((t, t), dtype=bool)), diff, -jnp.inf)


def _ssd(x, a, bm, cm, h0):
    b, l, g, r, p = x.shape
    n = bm.shape[-1]
    nc = l // SSD_CHUNK
    x = x.reshape(b, nc, SSD_CHUNK, g, r, p)
    bm = bm.reshape(b, nc, SSD_CHUNK, g, n)
    cm = cm.reshape(b, nc, SSD_CHUNK, g, n)
    a = jnp.moveaxis(a.reshape(b, nc, SSD_CHUNK, g, r), (3, 4), (1, 2))
    a_cs = jnp.cumsum(a, axis=-1)
    cb = jnp.einsum('bclgn,bcsgn->bgcls', cm, bm)
    m = cb[:, :, None] * jnp.exp(_segsum(a))
    y_diag = jnp.einsum('bgrcls,bcsgrp->bclgrp', m, x)
    decay_in = jnp.moveaxis(jnp.exp(a_cs[..., -1:] - a_cs), (1, 2), (3, 4))
    states = jnp.einsum('bclgn,bclgrp->bcgrpn', bm, x * decay_in[..., None])
    if h0 is None:
        h0 = jnp.zeros_like(states[:, 0])
    states = jnp.concatenate([h0[:, None], states], axis=1)
    chunk_decay = jnp.exp(_segsum(jnp.pad(a_cs[..., -1], ((0, 0), (0, 0), (0, 0), (1, 0)))))
    states = jnp.einsum('bgrzc,bcgrpn->bzgrpn', chunk_decay, states)
    decay_out = jnp.moveaxis(jnp.exp(a_cs), (1, 2), (3, 4))
    y_off = jnp.einsum('bclgn,bcgrpn->bclgrp', cm, states[:, :-1]) * decay_out[..., None]
    return (y_diag + y_off).reshape(b, l, g, r, p), states[:, -1]


def _ssd_mixer(u, uc, w_in, conv_w, conv_b, dt_bias, a_log, d_skip, norm_w, w_out, ctx_out):
    def prep(v):
        bsz, l = v.shape[:2]
        p = v @ w_in
        z = p[..., :SSD_INNER]
        xbc = jax.nn.silu(_dwconv(p[..., SSD_INNER:SSD_INNER + SSD_CONV_CH], conv_w, conv_b)).astype(F32)
        xs = xbc[..., :SSD_INNER].reshape(bsz, l, SSD_GROUPS, SSD_HPG, SSD_HEAD_DIM)
        bm = xbc[..., SSD_INNER:SSD_INNER + SSD_GROUPS * SSD_STATE].reshape(bsz, l, SSD_GROUPS, SSD_STATE)
        cm = xbc[..., SSD_INNER + SSD_GROUPS * SSD_STATE:].reshape(bsz, l, SSD_GROUPS, SSD_STATE)
        dt_raw = p[..., SSD_INNER + SSD_CONV_CH:].astype(F32).reshape(bsz, l, 2, SSD_GROUPS, SSD_HPG)
        return z, xs, bm, cm, dt_raw

    z, xs, bm, cm, dt_raw = prep(u)
    zc, xsc, bmc, cmc, dtc_raw = prep(uc)
    d_h = d_skip.astype(F32).reshape(SSD_GROUPS, SSD_HPG)[..., None]
    y = xs * d_h
    yc = xsc * d_h
    for d, rev in enumerate((False, True)):
        a_h = -jnp.exp(a_log[d].astype(F32)).reshape(SSD_GROUPS, SSD_HPG)
        bias = dt_bias[d].astype(F32).reshape(SSD_GROUPS, SSD_HPG)
        dt = jax.nn.softplus(dt_raw[:, :, d] + bias)
        dtc = jax.nn.softplus(dtc_raw[:, :, d] + bias)
        yc_d, h_ctx = _ssd(_flip(xsc * dtc[..., None], rev), _flip(dtc * a_h, rev),
                           _flip(bmc, rev), _flip(cmc, rev), None)
        y_d, _ = _ssd(_flip(xs * dt[..., None], rev), _flip(dt * a_h, rev),
                      _flip(bm, rev), _flip(cm, rev), h_ctx)
        y = y + _flip(y_d, rev)
        if ctx_out:
            yc = yc + _flip(yc_d, rev)

    def finish(yv, zv):
        bsz, l = yv.shape[:2]
        g = yv.reshape(bsz, l, SSD_INNER) * jax.nn.silu(zv.astype(F32))
        return _rms(g, norm_w).astype(zv.dtype) @ w_out

    return finish(y, z), (finish(yc, zc) if ctx_out else None)


def _mlp(v, w_up, w_down):
    return jnp.square(jax.nn.relu(v @ w_up)) @ w_down


def setup_inputs(seed: int = 0) -> dict:
    key = jax.random.key(seed)
    ks = iter(jax.random.split(key, 48))
    n_even = (DEPTH + 1) // 2
    n_odd = DEPTH // 2

    def nrm(shape, scale):
        return jax.random.normal(next(ks), shape, F32) * scale

    def unif(shape, lo, hi):
        return jax.random.uniform(next(ks), shape, F32, minval=lo, maxval=hi)

    d = D_MODEL
    ssd_dt = jnp.exp(unif((n_odd, 2, SSD_HEADS), math.log(1e-3), math.log(1e-1)))
    return {
        'x': nrm((BATCH, SEQ, d), 1.0),
        'c': nrm((BATCH, d), 1.0),
        'ctx': nrm((BATCH, CTX_LEN, d), 1.0),
        'c_ctx': nrm((d,), 1.0),
        'w_mod': nrm((DEPTH, d, N_MOD * d), 0.5 * d ** -0.5),
        'b_mod': nrm((DEPTH, N_MOD * d), 0.01),
        'norm_g': 1.0 + nrm((DEPTH, 4, d), 0.05),
        'w_in_even': nrm((n_even, d, EVEN_IN_W), d ** -0.5),
        'w_out_even': nrm((n_even, EVEN_OUT_W, d), EVEN_OUT_W ** -0.5),
        'diff_lam': nrm((n_even, 4, DIFF_QK_DIM), 0.1),
        'diff_subln': 1.0 + nrm((n_even, DIFF_V_DIM), 0.05),
        's5_lam_re': -0.5 + nrm((n_even, 2, S5_GROUPS, S5_STATE), 0.01),
        's5_lam_im': math.pi * jnp.arange(S5_STATE, dtype=F32) + nrm((n_even, 2, S5_GROUPS, S5_STATE), 0.01),
        's5_log_dt': unif((n_even, 2, S5_GROUPS), math.log(1e-3), math.log(1e-1)),
        's5_b_re': nrm((n_even, 2, S5_GROUPS, S5_STATE, S5_P), (2 * S5_P) ** -0.5),
        's5_b_im': nrm((n_even, 2, S5_GROUPS, S5_STATE, S5_P), (2 * S5_P) ** -0.5),
        's5_c_re': nrm((n_even, 2, S5_GROUPS, S5_P, S5_STATE), S5_STATE ** -0.5),
        's5_c_im': nrm((n_even, 2, S5_GROUPS, S5_P, S5_STATE), S5_STATE ** -0.5),
        's5_d': nrm((n_even, S5_GROUPS, S5_P), 0.5),
        's5_w_glu': nrm((n_even, S5_WIDTH, S5_WIDTH), S5_WIDTH ** -0.5),
        's5_b_glu': nrm((n_even, S5_WIDTH), 0.01),
        'w_in_odd': nrm((n_odd, d, ODD_IN_W), d ** -0.5),
        'conv_w': nrm((n_odd, SSD_CONV, SSD_CONV_CH), SSD_CONV ** -0.5),
        'conv_b': nrm((n_odd, SSD_CONV_CH), 0.01),
        'ssd_dt_bias': ssd_dt + jnp.log(-jnp.expm1(-ssd_dt)),
        'ssd_a_log': jnp.log(unif((n_odd, 2, SSD_HEADS), 1.0, 16.0)),
        'ssd_d': 1.0 + nrm((n_odd, SSD_HEADS), 0.1),
        'ssd_norm_w': 1.0 + nrm((n_odd, SSD_INNER), 0.05),
        'w_out_odd': nrm((n_odd, SSD_INNER, d), SSD_INNER ** -0.5),
        'w_up': nrm((DEPTH, d, MLP_HIDDEN), d ** -0.5),
        'w_down': nrm((DEPTH, MLP_HIDDEN, d), MLP_HIDDEN ** -0.5),
    }


def reference(x, c, ctx, c_ctx, w_mod, b_mod, norm_g, w_in_even, w_out_even, diff_lam, diff_subln,
              s5_lam_re, s5_lam_im, s5_log_dt, s5_b_re, s5_b_im, s5_c_re, s5_c_im, s5_d, s5_w_glu,
              s5_b_glu, w_in_odd, conv_w, conv_b, ssd_dt_bias, ssd_a_log, ssd_d, ssd_norm_w, w_out_odd,
              w_up, w_down):
    cos, sin = _axial_rope_tables(x.shape[1])
    h, hc = x, ctx
    for i in range(DEPTH):
        last = i == DEPTH - 1
        mod = [m[:, None, :] for m in jnp.split(jax.nn.silu(c) @ w_mod[i] + b_mod[i], N_MOD, axis=-1)]
        mod_c = jnp.split(jax.nn.silu(c_ctx) @ w_mod[i] + b_mod[i], N_MOD, axis=-1)
        g = norm_g[i]
        u = _modulate(_rms(h, g[0]), mod[0], mod[1])
        uc = _modulate(_rms(hc, g[0]), mod_c[0], mod_c[1])
        j = i // 2
        if i % 2 == 0:
            lam_init = 0.8 - 0.6 * math.exp(-0.3 * i)
            y, yc = _even_mixer(u, uc, w_in_even[j], w_out_even[j], diff_lam[j], diff_subln[j],
                                s5_lam_re[j], s5_lam_im[j], s5_log_dt[j], s5_b_re[j], s5_b_im[j],
                                s5_c_re[j], s5_c_im[j], s5_d[j], s5_w_glu[j], s5_b_glu[j],
                                cos, sin, lam_init, not last)
        else:
            y, yc = _ssd_mixer(u, uc, w_in_odd[j], conv_w[j], conv_b[j], ssd_dt_bias[j], ssd_a_log[j],
                               ssd_d[j], ssd_norm_w[j], w_out_odd[j], not last)
        h = h + mod[2] * _rms(y, g[1])
        h = h + mod[5] * _rms(_mlp(_modulate(_rms(h, g[2]), mod[3], mod[4]), w_up[i], w_down[i]), g[3])
        if not last:
            hc = hc + mod_c[2] * _rms(yc, g[1])
            hc = hc + mod_c[5] * _rms(_mlp(_modulate(_rms(hc, g[2]), mod_c[3], mod_c[4]), w_up[i], w_down[i]), g[3])
    return h
```

```python
import functools
import math

import jax
import jax.numpy as jnp
from jax import lax
from jax.experimental import pallas as pl
from jax.experimental.pallas import tpu as pltpu

F32 = jnp.float32
BF16 = jnp.bfloat16

D_MODEL = 2048
DEPTH = 4
GRID_W = 64
N_MOD = 6
RMS_EPS = 1e-6

DIFF_HEADS = 8
DIFF_QK_DIM = 64
DIFF_V_DIM = 128
DIFF_QK_W = 1024
DIFF_V_W = 1024
ROPE_BASE = 10000.0
ROPE_AXIS_DIM = 32

S5_WIDTH = 1024
S5_P = 16
S5_GROUPS = 64
S5_STATE = 64
S5_T = 16
S5_ROW = S5_T * S5_P

SSD_INNER = 4096
SSD_HEAD_DIM = 64
SSD_HEADS = 64
SSD_GROUPS = 8
SSD_HPG = 8
SSD_STATE = 128
SSD_CONV = 5
SSD_CHUNK = 128
SSD_CONV_CH = 6144
ODD_IN_W = 10368

MLP_HIDDEN = 8192

LANE = 128
VMEM_LIMIT = 56 * 1024 * 1024


def _tile(n, target, mult):
    best = None
    for cand in range(mult, min(n, target) + 1, mult):
        if n % cand == 0:
            best = cand
    assert best is not None, (n, target, mult)
    return best


def _cparams(sem):
    return pltpu.CompilerParams(dimension_semantics=sem, vmem_limit_bytes=VMEM_LIMIT)


def _rms_rows(x, g):
    return x * lax.rsqrt(jnp.mean(x * x, axis=-1, keepdims=True) + RMS_EPS) * g


def _pick_rows(sel_ctx, ref):
    return jnp.where(sel_ctx, ref[0:1, :], ref[1:2, :])


def _mod_kernel(a_ref, w_ref, b_ref, o_ref):
    a = a_ref[...]
    a = a * jax.nn.sigmoid(a)
    o_ref[0] = jnp.dot(a, w_ref[0], preferred_element_type=F32,
                       precision=lax.Precision.HIGHEST) + b_ref[0]


def _modulation(c_rows, w_mod, b_mod):
    depth, d, n = w_mod.shape
    tn = 1024
    return pl.pallas_call(
        _mod_kernel,
        grid=(depth, n // tn),
        in_specs=[pl.BlockSpec((8, d), lambda l, j: (0, 0)),
                  pl.BlockSpec((1, d, tn), lambda l, j: (l, 0, j)),
                  pl.BlockSpec((1, 1, tn), lambda l, j: (l, 0, j))],
        out_specs=pl.BlockSpec((1, 8, tn), lambda l, j: (l, 0, j)),
        out_shape=jax.ShapeDtypeStruct((depth, 8, n), F32),
        compiler_params=_cparams(("arbitrary", "arbitrary")),
        name="modulation",
    )(c_rows, w_mod, b_mod.reshape(depth, 1, n))


def _normmod_matmul_kernel(h_ref, g_ref, shift_ref, scale_ref, w_ref, o_ref, a_scr, *, n_ctx, tm, relu2):
    i = pl.program_id(0)

    @pl.when(pl.program_id(1) == 0)
    def _():
        y = _rms_rows(h_ref[...], g_ref[...])
        row = i * tm + lax.broadcasted_iota(jnp.int32, (tm, 1), 0)
        is_ctx = row < n_ctx
        a_scr[...] = (y * (1.0 + _pick_rows(is_ctx, scale_ref)) + _pick_rows(is_ctx, shift_ref)).astype(BF16)

    acc = jnp.dot(a_scr[...], w_ref[...].astype(BF16), preferred_element_type=F32)
    if relu2:
        acc = jnp.square(jnp.maximum(acc, 0.0))
    o_ref[...] = acc.astype(o_ref.dtype)


def _normmod_matmul(h, g, shift, scale, w, *, n_ctx, tm, tn, relu2, out_dtype, name):
    m, d = h.shape
    n = w.shape[1]
    return pl.pallas_call(
        functools.partial(_normmod_matmul_kernel, n_ctx=n_ctx, tm=tm, relu2=relu2),
        grid=(m // tm, n // tn),
        in_specs=[pl.BlockSpec((tm, d), lambda i, j: (i, 0)),
                  pl.BlockSpec((1, d), lambda i, j: (0, 0)),
                  pl.BlockSpec((2, d), lambda i, j: (0, 0)),
                  pl.BlockSpec((2, d), lambda i, j: (0, 0)),
                  pl.BlockSpec((d, tn), lambda i, j: (0, j))],
        out_specs=pl.BlockSpec((tm, tn), lambda i, j: (i, j)),
        out_shape=jax.ShapeDtypeStruct((m, n), out_dtype),
        scratch_shapes=[pltpu.VMEM((tm, d), BF16)],
        compiler_params=_cparams(("arbitrary", "arbitrary")),
        name=name,
    )(h, g.reshape(1, d), shift, scale, w)


def _matmul_resid_kernel(a_ref, w_ref, h_ref, g_ref, gate_ref, o_ref, *, n_ctx, tm, nk):
    i = pl.program_id(0)
    k = pl.program_id(1)
    part = jnp.dot(a_ref[...], w_ref[...].astype(BF16), preferred_element_type=F32)

    @pl.when(k == 0)
    def _():
        o_ref[...] = part

    @pl.when(k > 0)
    def _():
        o_ref[...] += part

    @pl.when(k == nk - 1)
    def _():
        row = i * tm + lax.broadcasted_iota(jnp.int32, (tm, 1), 0)
        gate = _pick_rows(row < n_ctx, gate_ref)
        o_ref[...] = h_ref[...] + gate * _rms_rows(o_ref[...], g_ref[...])


def _matmul_resid(a, w, h, g, gate, *, n_ctx, tm, tk, name):
    m, kdim = a.shape
    d = w.shape[1]
    nk = kdim // tk
    return pl.pallas_call(
        functools.partial(_matmul_resid_kernel, n_ctx=n_ctx, tm=tm, nk=nk),
        grid=(m // tm, nk),
        in_specs=[pl.BlockSpec((tm, tk), lambda i, k: (i, k)),
                  pl.BlockSpec((tk, d), lambda i, k: (k, 0)),
                  pl.BlockSpec((tm, d), lambda i, k: (i, 0)),
                  pl.BlockSpec((1, d), lambda i, k: (0, 0)),
                  pl.BlockSpec((2, d), lambda i, k: (0, 0))],
        out_specs=pl.BlockSpec((tm, d), lambda i, k: (i, 0)),
        out_shape=jax.ShapeDtypeStruct((m, d), F32),
        compiler_params=_cparams(("arbitrary", "arbitrary")),
        name=name,
    )(a, w, h, g.reshape(1, d), gate)


def _rope_tables(n_ctx, seq_len):
    rows = seq_len // GRID_W
    row = jnp.repeat(jnp.arange(rows, dtype=F32), GRID_W)
    col = jnp.tile(jnp.arange(GRID_W, dtype=F32), rows)
    inv = ROPE_BASE ** (-jnp.arange(0, ROPE_AXIS_DIM, 2, dtype=F32) / ROPE_AXIS_DIM)
    ang_r = row[:, None] * inv
    ang_c = col[:, None] * inv
    ang = jnp.concatenate([ang_r, ang_r, ang_c, ang_c], axis=-1)
    cos = jnp.concatenate([jnp.ones((n_ctx, DIFF_QK_DIM), F32), jnp.cos(ang)], axis=0)
    sin = jnp.concatenate([jnp.zeros((n_ctx, DIFF_QK_DIM), F32), jnp.sin(ang)], axis=0)
    sign = jnp.where((jnp.arange(DIFF_QK_DIM) % 32) < 16, -1.0, 1.0).astype(F32)
    sin = sin * sign
    return jnp.tile(cos, (1, 2)), jnp.tile(sin, (1, 2))


def _rope_kernel(p_ref, cos_ref, sin_ref, q_ref, k_ref, v_ref):
    cos = cos_ref[...]
    sin = sin_ref[...]
    low = (lax.broadcasted_iota(jnp.int32, (1, LANE), 1) % 32) < 16

    def rope(x):
        rot = jnp.where(low, pltpu.roll(x, LANE - 16, axis=1), pltpu.roll(x, 16, axis=1))
        return x * cos + rot * sin

    for c in range(DIFF_QK_W // LANE):
        sl = slice(c * LANE, (c + 1) * LANE)
        q_ref[:, sl] = (rope(p_ref[:, sl]) * (DIFF_QK_DIM ** -0.5)).astype(BF16)
        k_ref[:, sl] = rope(p_ref[:, DIFF_QK_W + c * LANE:DIFF_QK_W + (c + 1) * LANE]).astype(BF16)
    v_ref[...] = p_ref[:, 2 * DIFF_QK_W:2 * DIFF_QK_W + DIFF_V_W].astype(BF16)


def _rope_split(p, cos, sin, *, tm):
    m = p.shape[0]
    w = 2 * DIFF_QK_W + DIFF_V_W
    out = jax.ShapeDtypeStruct((m, DIFF_QK_W), BF16)
    return pl.pallas_call(
        _rope_kernel,
        grid=(m // tm,),
        in_specs=[pl.BlockSpec((tm, w), lambda i: (i, 0)),
                  pl.BlockSpec((tm, LANE), lambda i: (i, 0)),
                  pl.BlockSpec((tm, LANE), lambda i: (i, 0))],
        out_specs=[pl.BlockSpec((tm, DIFF_QK_W), lambda i: (i, 0))] * 3,
        out_shape=[out, out, out],
        compiler_params=_cparams(("arbitrary",)),
        name="rope_split",
    )(p, cos, sin)


def _attn_kernel(lam_ref, q0_ref, q1_ref, k0_ref, k1_ref, v_ref, subln_ref, o_ref,
                 m_scr, l_scr, acc_scr, *, tk, nk, out_scale):
    tq = q0_ref.shape[0]
    lam = lam_ref[0, 0]
    lane_head = lax.broadcasted_iota(jnp.int32, (1, LANE), 1) // DIFF_QK_DIM
    nt = (((1,), (1,)), ((), ()))
    for sub in range(2):
        qs = [jnp.where(lane_head == sub, q_ref[...], jnp.zeros((), BF16)) for q_ref in (q0_ref, q1_ref)]
        m_scr[...] = jnp.full(m_scr.shape, -jnp.inf, F32)
        l_scr[...] = jnp.zeros(l_scr.shape, F32)
        acc_scr[...] = jnp.zeros(acc_scr.shape, F32)

        def body(kb, carry):
            ks = pl.ds(pl.multiple_of(kb * tk, tk), tk)
            v = v_ref[ks, sub * DIFF_V_DIM:(sub + 1) * DIFF_V_DIM]
            for mp, k_ref in enumerate((k0_ref, k1_ref)):
                s = lax.dot_general(qs[mp], k_ref[ks, :], nt, preferred_element_type=F32)
                m_old = m_scr[mp]
                m_new = jnp.maximum(m_old, jnp.max(s, axis=-1, keepdims=True))
                alpha = jnp.exp(m_old - m_new)
                p = jnp.exp(s - m_new)
                l_scr[mp] = alpha * l_scr[mp] + jnp.sum(p, axis=-1, keepdims=True)
                acc_scr[mp] = alpha * acc_scr[mp] + jnp.dot(p.astype(BF16), v, preferred_element_type=F32)
                m_scr[mp] = m_new
            return carry

        lax.fori_loop(0, nk, body, 0)
        o = acc_scr[0] / l_scr[0] - lam * (acc_scr[1] / l_scr[1])
        o = _rms_rows(o, subln_ref[...]) * out_scale
        o_ref[:, sub * DIFF_V_DIM:(sub + 1) * DIFF_V_DIM] = o.astype(o_ref.dtype)


def _diff_attention(lam, q, k, v, subln, *, q_row0, n_q, n_k, tq, tk, lam_init, name):
    pairs = DIFF_HEADS // 2
    qb0 = q_row0 // tq
    nk = n_k // tk
    kern = functools.partial(_attn_kernel, tk=tk, nk=nk, out_scale=1.0 - lam_init)
    return pl.pallas_call(
        kern,
        grid=(pairs, n_q // tq),
        in_specs=[pl.BlockSpec(memory_space=pltpu.SMEM),
                  pl.BlockSpec((tq, LANE), lambda j, i: (qb0 + i, j)),
                  pl.BlockSpec((tq, LANE), lambda j, i: (qb0 + i, pairs + j)),
                  pl.BlockSpec((n_k, LANE), lambda j, i: (0, j)),
                  pl.BlockSpec((n_k, LANE), lambda j, i: (0, pairs + j)),
                  pl.BlockSpec((n_k, 2 * DIFF_V_DIM), lambda j, i: (0, j)),
                  pl.BlockSpec((1, DIFF_V_DIM), lambda j, i: (0, 0))],
        out_specs=pl.BlockSpec((tq, 2 * DIFF_V_DIM), lambda j, i: (i, j)),
        out_shape=jax.ShapeDtypeStruct((n_q, DIFF_V_W), BF16),
        scratch_shapes=[pltpu.VMEM((2, tq, 1), F32), pltpu.VMEM((2, tq, 1), F32),
                        pltpu.VMEM((2, tq, DIFF_V_DIM), F32)],
        compiler_params=_cparams(("arbitrary", "arbitrary")),
        name=name,
    )(lam, q, q, k, k, v, subln.reshape(1, DIFF_V_DIM))


def _s5_operators(lam_re, lam_im, log_dt, b_re, b_im, c_re, c_im, d_skip):
    t = S5_T
    steps = jnp.arange(t + 1, dtype=F32)
    ops = {}
    m_all = None
    for d in range(2):
        lam = lax.complex(lam_re[d].astype(F32), lam_im[d].astype(F32))
        dt = jnp.exp(log_dt[d].astype(F32))[:, None]
        lam_dt = lam * dt
        lam_bar = jnp.exp(lam_dt)
        b_bar = ((lam_bar - 1) / lam)[..., None] * lax.complex(b_re[d].astype(F32), b_im[d].astype(F32))
        c_mat = lax.complex(c_re[d].astype(F32), c_im[d].astype(F32))
        pw = jnp.exp(lam_dt[None] * steps[:, None, None])
        kern = jnp.real(jnp.einsum('gpn,tgn,gnq->tgpq', c_mat, pw[:t], b_bar,
                                   precision=lax.Precision.HIGHEST))
        s_idx = jnp.arange(t)[:, None]
        t_idx = jnp.arange(t)[None, :]
        lag = (t_idx - s_idx) if d == 0 else (s_idx - t_idx)
        blk = jnp.where((lag >= 0)[:, :, None, None, None], kern[jnp.clip(lag, 0, t - 1)], 0.0)
        m_d = jnp.transpose(blk, (2, 0, 4, 1, 3)).reshape(S5_GROUPS, S5_ROW, S5_ROW)
        m_all = m_d if m_all is None else m_all + m_d
        e_in = (t - 1 - jnp.arange(t)) if d == 0 else jnp.arange(t)
        vin = pw[e_in][:, :, :, None] * b_bar[None]
        vin = jnp.transpose(vin, (1, 0, 3, 2)).reshape(S5_GROUPS, S5_ROW, S5_STATE)
        v_d = jnp.concatenate([jnp.real(vin), jnp.imag(vin)], axis=-1)
        e_out = (jnp.arange(t) + 1) if d == 0 else (t - jnp.arange(t))
        wout = c_mat[None] * pw[e_out][:, :, None, :]
        wout = jnp.transpose(wout, (1, 3, 0, 2)).reshape(S5_GROUPS, S5_STATE, S5_ROW)
        w_d = jnp.concatenate([jnp.real(wout), -jnp.imag(wout)], axis=1)
        kk = (2.0 ** jnp.arange(10, dtype=F32)) * t
        ak = jnp.exp(lam_dt[:, None, :] * kk[None, :, None])
        a1 = jnp.concatenate([jnp.real(ak), jnp.real(ak)], axis=-1)
        a2 = jnp.concatenate([-jnp.imag(ak), jnp.imag(ak)], axis=-1)
        ops[d] = (v_d, w_d, a1, a2)
    eye = jnp.eye(S5_ROW, dtype=F32)
    dvec = jnp.tile(d_skip.astype(F32), (1, t))
    m_all = m_all + eye[None] * dvec[:, None, :]
    v_all = jnp.concatenate([ops[0][0], ops[1][0]], axis=-1)
    w_all = jnp.concatenate([ops[0][1], ops[1][1]], axis=1)
    pw_all = jnp.stack([ops[0][2], ops[0][3], ops[1][2], ops[1][3]], axis=1)
    return m_all.astype(BF16), v_all.astype(BF16), w_all.astype(BF16), pw_all


def _shift_rows(x, s, up):
    n = x.shape[0]
    if s >= n:
        return jnp.zeros_like(x)
    if s % 8 == 0:
        z = jnp.zeros((s, x.shape[1]), x.dtype)
        return jnp.concatenate([x[s:], z], axis=0) if up else jnp.concatenate([z, x[:n - s]], axis=0)
    row = lax.broadcasted_iota(jnp.int32, x.shape, 0)
    if up:
        return jnp.where(row < n - s, pltpu.roll(x, n - s, axis=0), 0.0)
    return jnp.where(row >= s, pltpu.roll(x, s, axis=0), 0.0)


def _s5_chunk_scan(z, a1_ref, a2_ref, up):
    x = _shift_rows(z, 1, up)
    n = x.shape[0]
    k = 0
    while (1 << k) < n:
        xs = _shift_rows(x, 1 << k, up)
        x = x + a1_ref[k:k + 1, :] * xs + a2_ref[k:k + 1, :] * pltpu.roll(xs, S5_STATE, axis=1)
        k += 1
    return x


def _s5_kernel(u_ref, m_ref, v_ref, w_ref, pw_ref, y_ref, *, gb, n_ctx_chunks):
    nc = n_ctx_chunks
    for g in range(gb):
        u = u_ref[g]
        y = jnp.dot(u, m_ref[g], preferred_element_type=F32)
        z = jnp.dot(u, v_ref[g], preferred_element_type=F32)
        zf = z[:, :LANE]
        zb = z[:, LANE:]
        sf = _s5_chunk_scan(zf, pw_ref.at[g, 0], pw_ref.at[g, 1], up=False)
        zb = jnp.concatenate([zb[nc:], zb[:nc]], axis=0)
        sb = _s5_chunk_scan(zb, pw_ref.at[g, 2], pw_ref.at[g, 3], up=True)
        nl = sb.shape[0] - nc
        sb = jnp.concatenate([sb[nl:], sb[:nl]], axis=0)
        s = jnp.concatenate([sf, sb], axis=1).astype(BF16)
        y_ref[g] = y + jnp.dot(s, w_ref[g], preferred_element_type=F32)


def _s5_scan(u_t, m_all, v_all, w_all, pw_all, *, gb, n_ctx_chunks):
    g, rows, _ = u_t.shape
    spec = pl.BlockSpec((gb, S5_ROW, S5_ROW), lambda i: (i, 0, 0))
    return pl.pallas_call(
        functools.partial(_s5_kernel, gb=gb, n_ctx_chunks=n_ctx_chunks),
        grid=(g // gb,),
        in_specs=[pl.BlockSpec((gb, rows, S5_ROW), lambda i: (i, 0, 0)), spec, spec, spec,
                  pl.BlockSpec((gb, 4, 10, LANE), lambda i: (i, 0, 0, 0))],
        out_specs=pl.BlockSpec((gb, rows, S5_ROW), lambda i: (i, 0, 0)),
        out_shape=jax.ShapeDtypeStruct((g, rows, S5_ROW), F32),
        compiler_params=_cparams(("arbitrary",)),
        name="s5_scan",
    )(u_t, m_all, v_all, w_all, pw_all)


def _glu_kernel(y_ref, w_ref, b_ref, o_ref):
    g = jax.nn.gelu(y_ref[...])
    gate = jnp.dot(g.astype(BF16), w_ref[...], preferred_element_type=F32) + b_ref[...]
    o_ref[...] = (g * jax.nn.sigmoid(gate)).astype(o_ref.dtype)


def _s5_glu(y, w_glu, b_glu, *, tm):
    m, w = y.shape
    return pl.pallas_call(
        _glu_kernel,
        grid=(m // tm,),
        in_specs=[pl.BlockSpec((tm, w), lambda i: (i, 0)),
                  pl.BlockSpec((w, w), lambda i: (0, 0)),
                  pl.BlockSpec((1, w), lambda i: (0, 0))],
        out_specs=pl.BlockSpec((tm, w), lambda i: (i, 0)),
        out_shape=jax.ShapeDtypeStruct((m, w), BF16),
        compiler_params=_cparams(("arbitrary",)),
        name="s5_glu",
    )(y, w_glu.astype(BF16), b_glu.reshape(1, w).astype(F32))


CONV_HALO = 8


def _conv_kernel(x_ref, w_ref, b_ref, o_ref, pad_scr, *, n_ctx, tile):
    m = x_ref.shape[0]
    half = SSD_CONV // 2
    zeros = jnp.zeros((CONV_HALO, LANE), F32)
    segs = ((0, n_ctx, CONV_HALO), (n_ctx, m, 2 * CONV_HALO))
    pad_scr[0:CONV_HALO, :] = zeros
    pad_scr[CONV_HALO + n_ctx:2 * CONV_HALO + n_ctx, :] = zeros
    pad_scr[2 * CONV_HALO + m:3 * CONV_HALO + m, :] = zeros
    for lo, hi, off in segs:
        pad_scr[lo + off:hi + off, :] = x_ref[lo:hi, :]
    bias = b_ref[...]
    for lo, hi, off in segs:
        for r0 in range(lo, hi, tile):
            acc = bias
            for tap in range(SSD_CONV):
                acc = acc + w_ref[tap:tap + 1, :] * pad_scr[r0 + off + tap - half:r0 + off + tap - half + tile, :]
            o_ref[r0:r0 + tile, :] = acc * jax.nn.sigmoid(acc)


def _conv_silu(p, conv_w, conv_b, *, n_ctx, col0):
    m = p.shape[0]
    cb0 = col0 // LANE
    return pl.pallas_call(
        functools.partial(_conv_kernel, n_ctx=n_ctx, tile=256),
        grid=(SSD_CONV_CH // LANE,),
        in_specs=[pl.BlockSpec((m, LANE), lambda j: (0, cb0 + j)),
                  pl.BlockSpec((SSD_CONV, LANE), lambda j: (0, j)),
                  pl.BlockSpec((1, LANE), lambda j: (0, j))],
        out_specs=pl.BlockSpec((m, LANE), lambda j: (0, j)),
        out_shape=jax.ShapeDtypeStruct((m, SSD_CONV_CH), F32),
        scratch_shapes=[pltpu.VMEM((m + 3 * CONV_HALO, LANE), F32)],
        compiler_params=_cparams(("arbitrary",)),
        name="conv_silu",
    )(p, conv_w, conv_b.reshape(1, SSD_CONV_CH))


def _split_dot(x, e):
    hi = x.astype(BF16)
    lo = (x - hi.astype(F32)).astype(BF16)
    return jnp.dot(hi, e, preferred_element_type=F32) + jnp.dot(lo, e, preferred_element_type=F32)


def _split3(x):
    x1 = x.astype(BF16)
    r1 = x - x1.astype(F32)
    x2 = r1.astype(BF16)
    x3 = (r1 - x2.astype(F32)).astype(BF16)
    return x1, x2, x3


def _ssd_kernel(x_ref, b_ref, c_ref, dt_ref, dtt_ref, bias_ref, biast_ref, a_ref, at_ref, tri_ref, e_ref,
                y_ref, state_scr):
    t = x_ref.shape[0]

    @pl.when(pl.program_id(1) == 0)
    def _():
        state_scr[...] = jnp.zeros(state_scr.shape, F32)

    tri = tri_ref[0]
    expand = e_ref[...]
    dt = jax.nn.softplus(dt_ref[0] + bias_ref[0])
    dtt = jax.nn.softplus(dtt_ref[0] + biast_ref[0])
    a = dt * a_ref[0]
    at = dtt * at_ref[0]
    cs = sum(jnp.dot(tri, ai, preferred_element_type=F32) for ai in _split3(a))
    cst = sum(lax.dot_general(ai, tri, (((1,), (1,)), ((), ())), preferred_element_type=F32)
              for ai in _split3(at))
    total = jnp.sum(a, axis=0, keepdims=True)
    dt_x = _split_dot(dt, expand)
    din_x = _split_dot(jnp.exp(total - cs), expand)
    dout_x = _split_dot(jnp.exp(cs), expand)
    cdec_x = _split_dot(jnp.broadcast_to(jnp.exp(total), (8, SSD_HEADS)), expand)[0:1]

    xdt = x_ref[...] * dt_x
    xin = (xdt * din_x).astype(BF16)
    xdt = xdt.astype(BF16)
    lane_head = lax.broadcasted_iota(jnp.int32, (1, LANE), 1) // SSD_HEAD_DIM
    gw = SSD_HPG * SSD_HEAD_DIM
    for g in range(SSD_GROUPS):
        bf = b_ref[:, g * SSD_STATE:(g + 1) * SSD_STATE]
        bg = bf.astype(BF16)
        bgt = bf.T.astype(BF16)
        cg = c_ref[:, g * SSD_STATE:(g + 1) * SSD_STATE].astype(BF16)
        cb = lax.dot_general(cg, bg, (((1,), (1,)), ((), ())), preferred_element_type=F32)
        gs = slice(g * gw, (g + 1) * gw)
        st = state_scr[:, gs]
        y_off = jnp.dot(cg, st.astype(BF16), preferred_element_type=F32) * dout_x[:, gs]
        state_scr[:, gs] = st * cdec_x[:, gs] + jnp.dot(bgt, xin[:, gs], preferred_element_type=F32)
        for pr in range(SSD_HPG // 2):
            cols = slice(g * gw + pr * LANE, g * gw + (pr + 1) * LANE)
            xp = xdt[:, cols]
            yd = None
            for sub in range(2):
                h = g * SSD_HPG + pr * 2 + sub
                seg = jnp.exp(cs[:, h:h + 1] - cst[h:h + 1, :])
                mm = jnp.where(tri > 0, cb * seg, 0.0).astype(BF16)
                xh = jnp.where(lane_head == sub, xp, jnp.zeros((), BF16))
                part = jnp.dot(mm, xh, preferred_element_type=F32)
                yd = part if yd is None else yd + part
            y_ref[0, :, cols] = yd + y_off[:, pr * LANE:(pr + 1) * LANE]


def _ssd_scan(xbc, dt, dtt, bias, a_neg, *, n_ctx):
    m = xbc.shape[0]
    t = SSD_CHUNK
    nchunks = m // t
    ctx_chunks = n_ctx // t

    def pos(d, c):
        back = jnp.where(c < ctx_chunks, ctx_chunks - 1 - c, nchunks - 1 + ctx_chunks - c)
        return jnp.where(d == 0, c, back)

    idx = jnp.arange(t)
    tri = jnp.stack([idx[:, None] >= idx[None, :], idx[:, None] <= idx[None, :]]).astype(BF16)
    expand = (jnp.arange(SSD_HEADS)[:, None] == (jnp.arange(SSD_INNER)[None, :] // SSD_HEAD_DIM)).astype(BF16)
    xb = SSD_INNER // 1024
    return pl.pallas_call(
        _ssd_kernel,
        grid=(2, nchunks),
        in_specs=[pl.BlockSpec((t, SSD_INNER), lambda d, c: (pos(d, c), 0)),
                  pl.BlockSpec((t, 1024), lambda d, c: (pos(d, c), xb)),
                  pl.BlockSpec((t, 1024), lambda d, c: (pos(d, c), xb + 1)),
                  pl.BlockSpec((1, t, SSD_HEADS), lambda d, c: (d, pos(d, c), 0)),
                  pl.BlockSpec((1, SSD_HEADS, t), lambda d, c: (d, 0, pos(d, c))),
                  pl.BlockSpec((1, 1, SSD_HEADS), lambda d, c: (d, 0, 0)),
                  pl.BlockSpec((1, SSD_HEADS, 1), lambda d, c: (d, 0, 0)),
                  pl.BlockSpec((1, 1, SSD_HEADS), lambda d, c: (d, 0, 0)),
                  pl.BlockSpec((1, SSD_HEADS, 1), lambda d, c: (d, 0, 0)),
                  pl.BlockSpec((1, t, t), lambda d, c: (d, 0, 0)),
                  pl.BlockSpec((SSD_HEADS, SSD_INNER), lambda d, c: (0, 0))],
        out_specs=pl.BlockSpec((1, t, SSD_INNER), lambda d, c: (d, pos(d, c), 0)),
        out_shape=jax.ShapeDtypeStruct((2, m, SSD_INNER), F32),
        scratch_shapes=[pltpu.VMEM((SSD_STATE, SSD_INNER), F32)],
        compiler_params=_cparams(("arbitrary", "arbitrary")),
        name="ssd_scan",
    )(xbc, xbc, xbc, dt, dtt, bias.reshape(2, 1, SSD_HEADS), bias.reshape(2, SSD_HEADS, 1),
      a_neg.reshape(2, 1, SSD_HEADS), a_neg.reshape(2, SSD_HEADS, 1), tri, expand)


def _ssd_finish_kernel(y_ref, x_ref, z_ref, d_ref, nw_ref, o_ref):
    z = z_ref[...]
    y = (x_ref[...] * d_ref[...] + y_ref[0] + y_ref[1]) * (z * jax.nn.sigmoid(z))
    o_ref[...] = _rms_rows(y, nw_ref[...]).astype(o_ref.dtype)


def _ssd_finish(y2, xbc, p, d_x, norm_w, *, tm):
    m = xbc.shape[0]
    w = SSD_INNER
    return pl.pallas_call(
        _ssd_finish_kernel,
        grid=(m // tm,),
        in_specs=[pl.BlockSpec((2, tm, w), lambda i: (0, i, 0)),
                  pl.BlockSpec((tm, w), lambda i: (i, 0)),
                  pl.BlockSpec((tm, w), lambda i: (i, 0)),
                  pl.BlockSpec((1, w), lambda i: (0, 0)),
                  pl.BlockSpec((1, w), lambda i: (0, 0))],
        out_specs=pl.BlockSpec((tm, w), lambda i: (i, 0)),
        out_shape=jax.ShapeDtypeStruct((m, w), BF16),
        compiler_params=_cparams(("arbitrary",)),
        name="ssd_finish",
    )(y2, xbc, p, d_x, norm_w.reshape(1, w))


def _even_mixer(hcat, g0, shift, scale, w_in, w_out, lam_p, subln, s5p, cos, sin, lam_init, *, n_ctx, tm):
    m = hcat.shape[0]
    p = _normmod_matmul(hcat, g0, shift, scale, w_in, n_ctx=n_ctx, tm=tm, tn=512, relu2=False,
                        out_dtype=F32, name="in_proj_even")
    q, k, v = _rope_split(p, cos, sin, tm=_tile(m, 256, 16))
    lp = lam_p.astype(F32)
    lam = (jnp.exp(jnp.sum(lp[0] * lp[1])) - jnp.exp(jnp.sum(lp[2] * lp[3])) + lam_init).reshape(1, 1)
    tq = _tile(math.gcd(n_ctx, m - n_ctx), 256, 16)
    o_lat = _diff_attention(lam, q, k, v, subln, q_row0=n_ctx, n_q=m - n_ctx, n_k=m, tq=tq,
                            tk=_tile(m, 768, 256), lam_init=lam_init, name="diff_attn")
    o_ctx = _diff_attention(lam, q, k, v, subln, q_row0=0, n_q=n_ctx, n_k=n_ctx, tq=tq,
                            tk=_tile(n_ctx, 768, 256), lam_init=lam_init, name="diff_attn_ctx")
    rows = m // S5_T
    us = p[:, 2 * DIFF_QK_W + DIFF_V_W:]
    u_t = us.reshape(rows, S5_T, S5_GROUPS, S5_P).transpose(2, 0, 1, 3).reshape(S5_GROUPS, rows, S5_ROW)
    y_t = _s5_scan(u_t.astype(BF16), *_s5_operators(*s5p[:8]), gb=8, n_ctx_chunks=n_ctx // S5_T)
    y = y_t.reshape(S5_GROUPS, rows, S5_T, S5_P).transpose(1, 2, 0, 3).reshape(m, S5_WIDTH)
    s = _s5_glu(y, s5p[8], s5p[9], tm=tm)
    return jnp.concatenate([jnp.concatenate([o_ctx, o_lat], axis=0), s], axis=-1)


def _odd_mixer(hcat, g0, shift, scale, w_in, conv_w, conv_b, dt_bias, a_log, d_skip, norm_w, *, n_ctx, tm):
    p = _normmod_matmul(hcat, g0, shift, scale, w_in, n_ctx=n_ctx, tm=tm, tn=384, relu2=False,
                        out_dtype=F32, name="in_proj_odd")
    xbc = _conv_silu(p, conv_w, conv_b, n_ctx=n_ctx, col0=SSD_INNER)
    dt_raw = p[:, SSD_INNER + SSD_CONV_CH:]
    dt = jnp.stack([dt_raw[:, :SSD_HEADS], dt_raw[:, SSD_HEADS:]])
    dtt = jnp.transpose(dt, (0, 2, 1))
    a_neg = -jnp.exp(a_log.astype(F32))
    y2 = _ssd_scan(xbc, dt, dtt, dt_bias.astype(F32), a_neg, n_ctx=n_ctx)
    d_x = jnp.repeat(d_skip.astype(F32), SSD_HEAD_DIM).reshape(1, SSD_INNER)
    return _ssd_finish(y2, xbc, p, d_x, norm_w, tm=_tile(xbc.shape[0], 176, 16))


def kernel(x, c, ctx, c_ctx, w_mod, b_mod, norm_g, w_in_even, w_out_even, diff_lam, diff_subln, s5_lam_re, s5_lam_im, s5_log_dt, s5_b_re, s5_b_im, s5_c_re, s5_c_im, s5_d, s5_w_glu, s5_b_glu, w_in_odd, conv_w, conv_b, ssd_dt_bias, ssd_a_log, ssd_d, ssd_norm_w, w_out_odd, w_up, w_down):
    bsz, seq, d = x.shape
    assert bsz == 1 and d == D_MODEL
    n_ctx = ctx.shape[1]
    hcat = jnp.concatenate([ctx[0], x[0]], axis=0)
    m = hcat.shape[0]
    tm = _tile(m, 1056, 16)
    tm_acc = _tile(m, 768, 16)
    c_rows = jnp.zeros((8, d), F32).at[0].set(c_ctx).at[1].set(c[0])
    mods = _modulation(c_rows, w_mod, b_mod)[:, :2].reshape(DEPTH, 2, N_MOD, d)
    cos, sin = _rope_tables(n_ctx, seq)
    for i in range(DEPTH):
        md = mods[i]
        g = norm_g[i]
        j = i // 2
        if i % 2 == 0:
            lam_init = 0.8 - 0.6 * math.exp(-0.3 * i)
            s5p = (s5_lam_re[j], s5_lam_im[j], s5_log_dt[j], s5_b_re[j], s5_b_im[j], s5_c_re[j], s5_c_im[j],
                   s5_d[j], s5_w_glu[j], s5_b_glu[j])
            mix = _even_mixer(hcat, g[0], md[:, 0], md[:, 1], w_in_even[j], w_out_even[j], diff_lam[j],
                              diff_subln[j], s5p, cos, sin, lam_init, n_ctx=n_ctx, tm=tm)
            w_out = w_out_even[j]
        else:
            mix = _odd_mixer(hcat, g[0], md[:, 0], md[:, 1], w_in_odd[j], conv_w[j], conv_b[j], ssd_dt_bias[j],
                             ssd_a_log[j], ssd_d[j], ssd_norm_w[j], n_ctx=n_ctx, tm=tm)
            w_out = w_out_odd[j]
        hcat = _matmul_resid(mix, w_out, hcat, g[1], md[:, 2], n_ctx=n_ctx, tm=tm_acc, tk=512, name="out_proj")
        hid = _normmod_matmul(hcat, g[2], md[:, 3], md[:, 4], w_up[i], n_ctx=n_ctx, tm=tm, tn=512, relu2=True,
                              out_dtype=BF16, name="mlp_up")
        hcat = _matmul_resid(hid, w_down[i], hcat, g[3], md[:, 5], n_ctx=n_ctx, tm=tm_acc, tk=512, name="mlp_down")
    return hcat[n_ctx:][None]
```

```python
import functools
import math

import jax
import jax.numpy as jnp
from jax import lax
from jax.experimental import pallas as pl
from jax.experimental.pallas import tpu as pltpu

F32 = jnp.float32
BF16 = jnp.bfloat16

D_MODEL = 2048
DEPTH = 4
GRID_W = 64
N_MOD = 6
RMS_EPS = 1e-6

DIFF_HEADS = 8
DIFF_QK_DIM = 64
DIFF_V_DIM = 128
DIFF_QK_W = 1024
DIFF_V_W = 1024
ROPE_BASE = 10000.0
ROPE_AXIS_DIM = 32
LOG2_E = math.log2(math.e)
VT_ROWS = DIFF_V_DIM + 16

S5_WIDTH = 1024
S5_P = 16
S5_GROUPS = 64
S5_STATE = 64
S5_T = 16
S5_ROW = S5_T * S5_P
S5_GB = 128 // S5_P

SSD_INNER = 4096
SSD_HEAD_DIM = 64
SSD_HEADS = 64
SSD_GROUPS = 8
SSD_HPG = 8
SSD_STATE = 128
SSD_CONV = 5
SSD_CHUNK = 128
SSD_CONV_CH = 6144
ODD_IN_W = 10368

MLP_HIDDEN = 8192

LANE = 128
VMEM_LIMIT = 56 * 1024 * 1024


def _tile(n, target, mult):
    best = None
    for cand in range(mult, min(n, target) + 1, mult):
        if n % cand == 0:
            best = cand
    assert best is not None, (n, target, mult)
    return best


def _cparams(sem):
    return pltpu.CompilerParams(dimension_semantics=sem, vmem_limit_bytes=VMEM_LIMIT)


def _rms_rows(x, g):
    return x * lax.rsqrt(jnp.mean(x * x, axis=-1, keepdims=True) + RMS_EPS) * g


def _pick_rows(sel_ctx, ref):
    return jnp.where(sel_ctx, ref[0:1, :], ref[1:2, :])


def _mod_kernel(a_ref, w_ref, b_ref, o_ref):
    a = a_ref[...]
    a = a * jax.nn.sigmoid(a)
    o_ref[0] = jnp.dot(a, w_ref[0], preferred_element_type=F32,
                       precision=lax.Precision.HIGHEST) + b_ref[0]


def _modulation(c_rows, w_mod, b_mod):
    depth, d, n = w_mod.shape
    tn = 1024
    return pl.pallas_call(
        _mod_kernel,
        grid=(depth, n // tn),
        in_specs=[pl.BlockSpec((8, d), lambda l, j: (0, 0)),
                  pl.BlockSpec((1, d, tn), lambda l, j: (l, 0, j)),
                  pl.BlockSpec((1, 1, tn), lambda l, j: (l, 0, j))],
        out_specs=pl.BlockSpec((1, 8, tn), lambda l, j: (l, 0, j)),
        out_shape=jax.ShapeDtypeStruct((depth, 8, n), F32),
        compiler_params=_cparams(("arbitrary", "arbitrary")),
        name="modulation",
    )(c_rows, w_mod, b_mod.reshape(depth, 1, n))


def _normmod_matmul_kernel(h_ref, g_ref, shift_ref, scale_ref, w_ref, o_ref, a_scr, *, ctx_rows, row_step, relu2):
    i = pl.program_id(0)

    @pl.when(pl.program_id(1) == 0)
    def _():
        y = _rms_rows(h_ref[...], g_ref[...])
        row = i * row_step + lax.broadcasted_iota(jnp.int32, (h_ref.shape[0], 1), 0)
        is_ctx = row < ctx_rows
        a_scr[...] = (y * (1.0 + _pick_rows(is_ctx, scale_ref)) + _pick_rows(is_ctx, shift_ref)).astype(BF16)

    acc = jnp.dot(a_scr[...], w_ref[...].astype(BF16), preferred_element_type=F32)
    if relu2:
        acc = jnp.square(jnp.maximum(acc, 0.0))
    o_ref[...] = acc.astype(o_ref.dtype)


def _normmod_matmul(h, g, shift, scale, w, *, n_ctx, tm, tn, relu2, out_dtype, name, cols=None, slots=None):
    m, d = h.shape
    col0, n = (0, w.shape[1]) if cols is None else cols
    cb0 = col0 // tn
    assert col0 % tn == 0 and n % tn == 0
    if slots is None:
        steps, ctx_rows, row_step = m // tm, n_ctx, tm
        h_spec = pl.BlockSpec((tm, d), lambda i, j: (i, 0))
    else:
        tm = m // slots
        steps, ctx_rows, row_step = slots, n_ctx // slots, 0
        h = h.reshape(tm, slots * d)
        h_spec = pl.BlockSpec((tm, d), lambda i, j: (0, i))
    return pl.pallas_call(
        functools.partial(_normmod_matmul_kernel, ctx_rows=ctx_rows, row_step=row_step, relu2=relu2),
        grid=(steps, n // tn),
        in_specs=[h_spec,
                  pl.BlockSpec((1, d), lambda i, j: (0, 0)),
                  pl.BlockSpec((2, d), lambda i, j: (0, 0)),
                  pl.BlockSpec((2, d), lambda i, j: (0, 0)),
                  pl.BlockSpec((d, tn), lambda i, j: (0, cb0 + j))],
        out_specs=pl.BlockSpec((tm, tn), lambda i, j: (i, j)),
        out_shape=jax.ShapeDtypeStruct((m, n), out_dtype),
        scratch_shapes=[pltpu.VMEM((tm, d), BF16)],
        compiler_params=_cparams(("arbitrary", "arbitrary")),
        name=name,
    )(h, g.reshape(1, d), shift, scale, w)


def _matmul_resid_kernel(a_ref, w_ref, h_ref, g_ref, gate_ref, o_ref, *, n_ctx, tm, nk):
    i = pl.program_id(0)
    k = pl.program_id(1)
    part = jnp.dot(a_ref[...], w_ref[...].astype(BF16), preferred_element_type=F32)

    @pl.when(k == 0)
    def _():
        o_ref[...] = part

    @pl.when(k > 0)
    def _():
        o_ref[...] += part

    @pl.when(k == nk - 1)
    def _():
        row = i * tm + lax.broadcasted_iota(jnp.int32, (tm, 1), 0)
        gate = _pick_rows(row < n_ctx, gate_ref)
        o_ref[...] = h_ref[...] + gate * _rms_rows(o_ref[...], g_ref[...])


def _matmul_resid(a, w, h, g, gate, *, n_ctx, tm, tk, name):
    m, kdim = a.shape
    d = w.shape[1]
    nk = kdim // tk
    return pl.pallas_call(
        functools.partial(_matmul_resid_kernel, n_ctx=n_ctx, tm=tm, nk=nk),
        grid=(m // tm, nk),
        in_specs=[pl.BlockSpec((tm, tk), lambda i, k: (i, k)),
                  pl.BlockSpec((tk, d), lambda i, k: (k, 0)),
                  pl.BlockSpec((tm, d), lambda i, k: (i, 0)),
                  pl.BlockSpec((1, d), lambda i, k: (0, 0)),
                  pl.BlockSpec((2, d), lambda i, k: (0, 0))],
        out_specs=pl.BlockSpec((tm, d), lambda i, k: (i, 0)),
        out_shape=jax.ShapeDtypeStruct((m, d), F32),
        compiler_params=_cparams(("arbitrary", "arbitrary")),
        name=name,
    )(a, w, h, g.reshape(1, d), gate)


def _rope_tables(n_ctx, seq_len):
    rows = seq_len // GRID_W
    row = jnp.repeat(jnp.arange(rows, dtype=F32), GRID_W)
    col = jnp.tile(jnp.arange(GRID_W, dtype=F32), rows)
    inv = ROPE_BASE ** (-jnp.arange(0, ROPE_AXIS_DIM, 2, dtype=F32) / ROPE_AXIS_DIM)
    ang_r = row[:, None] * inv
    ang_c = col[:, None] * inv
    ang = jnp.concatenate([ang_r, ang_r, ang_c, ang_c], axis=-1)
    cos = jnp.concatenate([jnp.ones((n_ctx, DIFF_QK_DIM), F32), jnp.cos(ang)], axis=0)
    sin = jnp.concatenate([jnp.zeros((n_ctx, DIFF_QK_DIM), F32), jnp.sin(ang)], axis=0)
    sign = jnp.where((jnp.arange(DIFF_QK_DIM) % 32) < 16, -1.0, 1.0).astype(F32)
    sin = sin * sign
    return jnp.tile(cos, (1, 2)), jnp.tile(sin, (1, 2))


def _rope_kernel(p_ref, cos_ref, sin_ref, q_ref, k_ref, vt_ref):
    cos = cos_ref[...]
    sin = sin_ref[...]
    low = (lax.broadcasted_iota(jnp.int32, (1, LANE), 1) % 32) < 16

    def rope(x):
        rot = jnp.where(low, pltpu.roll(x, LANE - 16, axis=1), pltpu.roll(x, 16, axis=1))
        return x * cos + rot * sin

    for c in range(DIFF_QK_W // LANE):
        sl = slice(c * LANE, (c + 1) * LANE)
        q_ref[:, sl] = (rope(p_ref[:, sl]) * (DIFF_QK_DIM ** -0.5 * LOG2_E)).astype(BF16)
        k_ref[:, sl] = rope(p_ref[:, DIFF_QK_W + c * LANE:DIFF_QK_W + (c + 1) * LANE]).astype(BF16)
    for h in range(DIFF_HEADS):
        v = p_ref[:, 2 * DIFF_QK_W + h * DIFF_V_DIM:2 * DIFF_QK_W + (h + 1) * DIFF_V_DIM]
        vt_ref[h, 0:DIFF_V_DIM, :] = v.T.astype(BF16)
        vt_ref[h, DIFF_V_DIM:VT_ROWS, :] = jnp.ones((VT_ROWS - DIFF_V_DIM, v.shape[0]), BF16)


def _rope_split(p, cos, sin, *, tm):
    m = p.shape[0]
    w = 2 * DIFF_QK_W + DIFF_V_W
    out = jax.ShapeDtypeStruct((m, DIFF_QK_W), BF16)
    return pl.pallas_call(
        _rope_kernel,
        grid=(m // tm,),
        in_specs=[pl.BlockSpec((tm, w), lambda i: (i, 0)),
                  pl.BlockSpec((tm, LANE), lambda i: (i, 0)),
                  pl.BlockSpec((tm, LANE), lambda i: (i, 0))],
        out_specs=[pl.BlockSpec((tm, DIFF_QK_W), lambda i: (i, 0)),
                   pl.BlockSpec((tm, DIFF_QK_W), lambda i: (i, 0)),
                   pl.BlockSpec((DIFF_HEADS, VT_ROWS, tm), lambda i: (0, 0, i))],
        out_shape=[out, out, jax.ShapeDtypeStruct((DIFF_HEADS, VT_ROWS, m), BF16)],
        compiler_params=_cparams(("arbitrary",)),
        name="rope_split",
    )(p, cos, sin)


def _attn_kernel(lam_ref, q0_ref, q1_ref, k0_ref, k1_ref, vt_ref, subln_ref, o_ref, acc_scr, s_scr,
                 *, tk, nk, out_scale):
    tq = q0_ref.shape[0]
    lam = lam_ref[0, 0]
    lane_head = lax.broadcasted_iota(jnp.int32, (1, LANE), 1) // DIFF_QK_DIM
    nt = (((1,), (1,)), ((), ()))
    k_refs = (k0_ref, k1_ref)
    for sub in range(2):
        qs = [jnp.where(lane_head == sub, q_ref[...], jnp.zeros((), BF16)) for q_ref in (q0_ref, q1_ref)]
        acc_scr[...] = jnp.zeros(acc_scr.shape, F32)

        def keys(kb):
            start = kb * tk
            return pl.ds(start if isinstance(kb, int) else pl.multiple_of(start, tk), tk)

        def scores(kb, slot):
            for mp in range(2):
                s_scr[slot, mp] = lax.dot_general(k_refs[mp][keys(kb), :], qs[mp], nt,
                                                  preferred_element_type=F32)

        def softmax_pv(kb, slot, ms):
            vt = vt_ref[sub, :, keys(kb)]
            new_ms = []
            for mp in range(2):
                s = s_scr[slot, mp]
                m_new = jnp.maximum(ms[mp], jnp.max(s, axis=0, keepdims=True))
                alpha = jnp.exp2(ms[mp] - m_new)
                p = jnp.exp2((s - m_new).astype(BF16))
                acc_scr[mp] = alpha * acc_scr[mp] + jnp.dot(vt, p, preferred_element_type=F32)
                new_ms.append(m_new)
            return tuple(new_ms)

        def pair(i, ms):
            scores(2 * i + 1, 1)
            ms = softmax_pv(2 * i, 0, ms)
            scores(2 * i + 2, 0)
            return softmax_pv(2 * i + 1, 1, ms)

        m_init = jnp.full((1, tq), -jnp.inf, F32)
        scores(0, 0)
        ms = lax.fori_loop(0, (nk - 1) // 2, pair, (m_init, m_init))
        if (nk - 1) % 2 == 1:
            scores(nk - 1, 1)
            ms = softmax_pv(nk - 2, 0, ms)
            softmax_pv(nk - 1, 1, ms)
        else:
            softmax_pv(nk - 1, 0, ms)
        o0 = acc_scr[0, 0:DIFF_V_DIM, :] / acc_scr[0, DIFF_V_DIM:DIFF_V_DIM + 1, :]
        o1 = acc_scr[1, 0:DIFF_V_DIM, :] / acc_scr[1, DIFF_V_DIM:DIFF_V_DIM + 1, :]
        o = o0 - lam * o1
        o = o * lax.rsqrt(jnp.mean(o * o, axis=0, keepdims=True) + RMS_EPS) * (subln_ref[...] * out_scale)
        o_ref[:, sub * DIFF_V_DIM:(sub + 1) * DIFF_V_DIM] = o.T.astype(o_ref.dtype)


def _diff_attention(lam, q, k, vt, subln, *, n_k, tq, tk, lam_init, name):
    pairs = DIFF_HEADS // 2
    n_q = q.shape[0]
    nk = n_k // tk
    kern = functools.partial(_attn_kernel, tk=tk, nk=nk, out_scale=1.0 - lam_init)
    return pl.pallas_call(
        kern,
        grid=(pairs, n_q // tq),
        in_specs=[pl.BlockSpec(memory_space=pltpu.SMEM),
                  pl.BlockSpec((tq, LANE), lambda j, i: (i, j)),
                  pl.BlockSpec((tq, LANE), lambda j, i: (i, pairs + j)),
                  pl.BlockSpec((n_k, LANE), lambda j, i: (0, j)),
                  pl.BlockSpec((n_k, LANE), lambda j, i: (0, pairs + j)),
                  pl.BlockSpec((2, VT_ROWS, n_k), lambda j, i: (j, 0, 0)),
                  pl.BlockSpec((DIFF_V_DIM, 1), lambda j, i: (0, 0))],
        out_specs=pl.BlockSpec((tq, 2 * DIFF_V_DIM), lambda j, i: (i, j)),
        out_shape=jax.ShapeDtypeStruct((n_q, DIFF_V_W), BF16),
        scratch_shapes=[pltpu.VMEM((2, VT_ROWS, tq), F32), pltpu.VMEM((2, 2, tk, tq), F32)],
        compiler_params=_cparams(("arbitrary", "arbitrary")),
        name=name,
    )(lam, q, q, k, k, vt, subln.reshape(DIFF_V_DIM, 1).astype(F32))


def _s5_operators(lam_re, lam_im, log_dt, b_re, b_im, c_re, c_im, d_skip):
    t = S5_T
    steps = jnp.arange(t + 1, dtype=F32)
    ops = {}
    m_all = None
    for d in range(2):
        lam = lax.complex(lam_re[d].astype(F32), lam_im[d].astype(F32))
        dt = jnp.exp(log_dt[d].astype(F32))[:, None]
        lam_dt = lam * dt
        lam_bar = jnp.exp(lam_dt)
        b_bar = ((lam_bar - 1) / lam)[..., None] * lax.complex(b_re[d].astype(F32), b_im[d].astype(F32))
        c_mat = lax.complex(c_re[d].astype(F32), c_im[d].astype(F32))
        pw = jnp.exp(lam_dt[None] * steps[:, None, None])
        kern = jnp.real(jnp.einsum('gpn,tgn,gnq->tgpq', c_mat, pw[:t], b_bar,
                                   precision=lax.Precision.HIGHEST))
        s_idx = jnp.arange(t)[:, None]
        t_idx = jnp.arange(t)[None, :]
        lag = (t_idx - s_idx) if d == 0 else (s_idx - t_idx)
        onehot = (lag[None] == jnp.arange(t)[:, None, None]).astype(F32)
        m_d = jnp.einsum('xst,xgpq->gsqtp', onehot, kern,
                         precision=lax.Precision.HIGHEST).reshape(S5_GROUPS, S5_ROW, S5_ROW)
        m_all = m_d if m_all is None else m_all + m_d
        pw_in = pw[:t][::-1] if d == 0 else pw[:t]
        vin = pw_in[:, :, :, None] * b_bar[None]
        vin = jnp.transpose(vin, (1, 0, 3, 2)).reshape(S5_GROUPS, S5_ROW, S5_STATE)
        v_d = jnp.concatenate([jnp.real(vin), jnp.imag(vin)], axis=-1)
        pw_out = pw[1:] if d == 0 else pw[1:][::-1]
        wout = c_mat[None] * pw_out[:, :, None, :]
        wout = jnp.transpose(wout, (1, 3, 0, 2)).reshape(S5_GROUPS, S5_STATE, S5_ROW)
        w_d = jnp.concatenate([jnp.real(wout), -jnp.imag(wout)], axis=1)
        kk = (2.0 ** jnp.arange(10, dtype=F32)) * t
        ak = jnp.exp(lam_dt[:, None, :] * kk[None, :, None])
        a1 = jnp.concatenate([jnp.real(ak), jnp.real(ak)], axis=-1)
        a2 = jnp.concatenate([-jnp.imag(ak), jnp.imag(ak)], axis=-1)
        ops[d] = (v_d, w_d, a1, a2)
    eye = jnp.eye(S5_ROW, dtype=F32)
    dvec = jnp.tile(d_skip.astype(F32), (1, t))
    m_all = m_all + eye[None] * dvec[:, None, :]
    v_all = jnp.concatenate([ops[0][0], ops[1][0]], axis=-1)
    w_all = jnp.concatenate([ops[0][1], ops[1][1]], axis=1)
    pw_all = jnp.stack([ops[0][2], ops[0][3], ops[1][2], ops[1][3]], axis=1)
    return m_all.astype(BF16), v_all.astype(BF16), w_all.astype(BF16), pw_all


def _shift_rows(x, s, up):
    n = x.shape[0]
    if s >= n:
        return jnp.zeros_like(x)
    if s % 8 == 0:
        z = jnp.zeros((s, x.shape[1]), x.dtype)
        return jnp.concatenate([x[s:], z], axis=0) if up else jnp.concatenate([z, x[:n - s]], axis=0)
    row = lax.broadcasted_iota(jnp.int32, x.shape, 0)
    if up:
        return jnp.where(row < n - s, pltpu.roll(x, n - s, axis=0), 0.0)
    return jnp.where(row >= s, pltpu.roll(x, s, axis=0), 0.0)


def _s5_chunk_scan(z, a1_ref, a2_ref, up):
    x = _shift_rows(z, 1, up)
    n = x.shape[0]
    k = 0
    while (1 << k) < n:
        xs = _shift_rows(x, 1 << k, up)
        x = x + a1_ref[k:k + 1, :] * xs + a2_ref[k:k + 1, :] * pltpu.roll(xs, S5_STATE, axis=1)
        k += 1
    return x


def _s5_regroup_matrix():
    n = S5_T * LANE
    src = jnp.arange(n)
    t, g, q = src // LANE, (src % LANE) // S5_P, src % S5_P
    dst = g * S5_ROW + t * S5_P + q
    return (dst[:, None] == jnp.arange(n)[None, :]).astype(BF16)


def _s5_kernel(x_ref, perm_ref, m_ref, v_ref, w_ref, pw_ref, y_ref, *, n_ctx_chunks):
    nc = n_ctx_chunks
    perm = perm_ref[...]
    x = jnp.concatenate([x_ref[t] for t in range(S5_T)], axis=1)
    xp = jnp.dot(x, perm, preferred_element_type=F32).astype(BF16)
    ys = []
    for g in range(S5_GB):
        u = xp[:, g * S5_ROW:(g + 1) * S5_ROW]
        y = jnp.dot(u, m_ref[g], preferred_element_type=F32)
        z = jnp.dot(u, v_ref[g], preferred_element_type=F32)
        zf = z[:, :LANE]
        zb = z[:, LANE:]
        sf = _s5_chunk_scan(zf, pw_ref.at[g, 0], pw_ref.at[g, 1], up=False)
        zb = jnp.concatenate([zb[nc:], zb[:nc]], axis=0)
        sb = _s5_chunk_scan(zb, pw_ref.at[g, 2], pw_ref.at[g, 3], up=True)
        nl = sb.shape[0] - nc
        sb = jnp.concatenate([sb[nl:], sb[:nl]], axis=0)
        s = jnp.concatenate([sf, sb], axis=1).astype(BF16)
        ys.append(y + jnp.dot(s, w_ref[g], preferred_element_type=F32))
    y = jnp.concatenate(ys, axis=1)
    hi = y.astype(BF16)
    lo = (y - hi.astype(F32)).astype(BF16)
    nt = (((1,), (1,)), ((), ()))
    yt = (lax.dot_general(hi, perm, nt, preferred_element_type=F32)
          + lax.dot_general(lo, perm, nt, preferred_element_type=F32))
    for t in range(S5_T):
        y_ref[t] = yt[:, t * LANE:(t + 1) * LANE]


def _s5_scan(x_t, m_all, v_all, w_all, pw_all, *, n_ctx_chunks):
    t, rows, width = x_t.shape
    spec = pl.BlockSpec((S5_GB, S5_ROW, S5_ROW), lambda i: (i, 0, 0))
    n = S5_T * LANE
    return pl.pallas_call(
        functools.partial(_s5_kernel, n_ctx_chunks=n_ctx_chunks),
        grid=(width // LANE,),
        in_specs=[pl.BlockSpec((t, rows, LANE), lambda i: (0, 0, i)),
                  pl.BlockSpec((n, n), lambda i: (0, 0)), spec, spec, spec,
                  pl.BlockSpec((S5_GB, 4, 10, LANE), lambda i: (i, 0, 0, 0))],
        out_specs=pl.BlockSpec((t, rows, LANE), lambda i: (0, 0, i)),
        out_shape=jax.ShapeDtypeStruct((t, rows, width), F32),
        compiler_params=_cparams(("arbitrary",)),
        name="s5_scan",
    )(x_t, _s5_regroup_matrix(), m_all, v_all, w_all, pw_all)


def _glu_kernel(y_ref, w_ref, b_ref, o_ref):
    g = jax.nn.gelu(y_ref[0])
    gate = jnp.dot(g.astype(BF16), w_ref[...], preferred_element_type=F32) + b_ref[...]
    o_ref[...] = (g * jax.nn.sigmoid(gate)).astype(o_ref.dtype)


def _s5_glu(y_t, w_glu, b_glu):
    t, rows, w = y_t.shape
    out = pl.pallas_call(
        _glu_kernel,
        grid=(t,),
        in_specs=[pl.BlockSpec((1, rows, w), lambda i: (i, 0, 0)),
                  pl.BlockSpec((w, w), lambda i: (0, 0)),
                  pl.BlockSpec((1, w), lambda i: (0, 0))],
        out_specs=pl.BlockSpec((rows, w), lambda i: (0, i)),
        out_shape=jax.ShapeDtypeStruct((rows, t * w), BF16),
        compiler_params=_cparams(("arbitrary",)),
        name="s5_glu",
    )(y_t, w_glu.astype(BF16), b_glu.reshape(1, w).astype(F32))
    return out.reshape(rows * t, w)


CONV_HALO = 8


def _conv_kernel(x_ref, w_ref, b_ref, o_ref, pad_scr, *, n_ctx, tile):
    m = x_ref.shape[0]
    half = SSD_CONV // 2
    zeros = jnp.zeros((CONV_HALO, LANE), F32)
    segs = ((0, n_ctx, CONV_HALO), (n_ctx, m, 2 * CONV_HALO))
    pad_scr[0:CONV_HALO, :] = zeros
    pad_scr[CONV_HALO + n_ctx:2 * CONV_HALO + n_ctx, :] = zeros
    pad_scr[2 * CONV_HALO + m:3 * CONV_HALO + m, :] = zeros
    for lo, hi, off in segs:
        pad_scr[lo + off:hi + off, :] = x_ref[lo:hi, :]
    bias = b_ref[...]
    for lo, hi, off in segs:
        for r0 in range(lo, hi, tile):
            acc = bias
            for tap in range(SSD_CONV):
                acc = acc + w_ref[tap:tap + 1, :] * pad_scr[r0 + off + tap - half:r0 + off + tap - half + tile, :]
            o_ref[r0:r0 + tile, :] = acc * jax.nn.sigmoid(acc)


def _conv_silu(p, conv_w, conv_b, *, n_ctx, col0):
    m = p.shape[0]
    cb0 = col0 // LANE
    return pl.pallas_call(
        functools.partial(_conv_kernel, n_ctx=n_ctx, tile=256),
        grid=(SSD_CONV_CH // LANE,),
        in_specs=[pl.BlockSpec((m, LANE), lambda j: (0, cb0 + j)),
                  pl.BlockSpec((SSD_CONV, LANE), lambda j: (0, j)),
                  pl.BlockSpec((1, LANE), lambda j: (0, j))],
        out_specs=pl.BlockSpec((m, LANE), lambda j: (0, j)),
        out_shape=jax.ShapeDtypeStruct((m, SSD_CONV_CH), F32),
        scratch_shapes=[pltpu.VMEM((m + 3 * CONV_HALO, LANE), F32)],
        compiler_params=_cparams(("arbitrary",)),
        name="conv_silu",
    )(p, conv_w, conv_b.reshape(1, SSD_CONV_CH))


def _split_dot(x, e):
    hi = x.astype(BF16)
    lo = (x - hi.astype(F32)).astype(BF16)
    return jnp.dot(hi, e, preferred_element_type=F32) + jnp.dot(lo, e, preferred_element_type=F32)


def _split3(x):
    x1 = x.astype(BF16)
    r1 = x - x1.astype(F32)
    x2 = r1.astype(BF16)
    x3 = (r1 - x2.astype(F32)).astype(BF16)
    return x1, x2, x3


def _ssd_kernel(x_ref, b_ref, c_ref, dt_ref, dtt_ref, bias_ref, biast_ref, a_ref, at_ref, tri_ref, e_ref,
                y_ref, state_scr):
    t = x_ref.shape[0]

    @pl.when(pl.program_id(1) == 0)
    def _():
        state_scr[...] = jnp.zeros(state_scr.shape, F32)

    tri = tri_ref[0]
    expand = e_ref[...]
    dt = jax.nn.softplus(dt_ref[0] + bias_ref[0])
    dtt = jax.nn.softplus(dtt_ref[0] + biast_ref[0])
    a = dt * a_ref[0]
    at = dtt * at_ref[0]
    cs = sum(jnp.dot(tri, ai, preferred_element_type=F32) for ai in _split3(a))
    cst = sum(lax.dot_general(ai, tri, (((1,), (1,)), ((), ())), preferred_element_type=F32)
              for ai in _split3(at))
    total = jnp.sum(a, axis=0, keepdims=True)
    dt_x = _split_dot(dt, expand)
    din_x = _split_dot(jnp.exp(total - cs), expand)
    dout_x = _split_dot(jnp.exp(cs), expand)
    cdec_x = _split_dot(jnp.broadcast_to(jnp.exp(total), (8, SSD_HEADS)), expand)[0:1]

    xdt = x_ref[...] * dt_x
    xin = (xdt * din_x).astype(BF16)
    xdt = xdt.astype(BF16)
    lane_head = lax.broadcasted_iota(jnp.int32, (1, LANE), 1) // SSD_HEAD_DIM
    gw = SSD_HPG * SSD_HEAD_DIM
    for g in range(SSD_GROUPS):
        bf = b_ref[:, g * SSD_STATE:(g + 1) * SSD_STATE]
        bg = bf.astype(BF16)
        bgt = bf.T.astype(BF16)
        cg = c_ref[:, g * SSD_STATE:(g + 1) * SSD_STATE].astype(BF16)
        cb = lax.dot_general(cg, bg, (((1,), (1,)), ((), ())), preferred_element_type=F32)
        gs = slice(g * gw, (g + 1) * gw)
        st = state_scr[:, gs]
        y_off = jnp.dot(cg, st.astype(BF16), preferred_element_type=F32) * dout_x[:, gs]
        state_scr[:, gs] = st * cdec_x[:, gs] + jnp.dot(bgt, xin[:, gs], preferred_element_type=F32)
        for pr in range(SSD_HPG // 2):
            cols = slice(g * gw + pr * LANE, g * gw + (pr + 1) * LANE)
            xp = xdt[:, cols]
            yd = None
            for sub in range(2):
                h = g * SSD_HPG + pr * 2 + sub
                seg = jnp.exp(cs[:, h:h + 1] - cst[h:h + 1, :])
                mm = jnp.where(tri > 0, cb * seg, 0.0).astype(BF16)
                xh = jnp.where(lane_head == sub, xp, jnp.zeros((), BF16))
                part = jnp.dot(mm, xh, preferred_element_type=F32)
                yd = part if yd is None else yd + part
            y_ref[0, :, cols] = yd + y_off[:, pr * LANE:(pr + 1) * LANE]


def _ssd_scan(xbc, dt, dtt, bias, a_neg, *, n_ctx):
    m = xbc.shape[0]
    t = SSD_CHUNK
    nchunks = m // t
    ctx_chunks = n_ctx // t

    def pos(d, c):
        back = jnp.where(c < ctx_chunks, ctx_chunks - 1 - c, nchunks - 1 + ctx_chunks - c)
        return jnp.where(d == 0, c, back)

    idx = jnp.arange(t)
    tri = jnp.stack([idx[:, None] >= idx[None, :], idx[:, None] <= idx[None, :]]).astype(BF16)
    expand = (jnp.arange(SSD_HEADS)[:, None] == (jnp.arange(SSD_INNER)[None, :] // SSD_HEAD_DIM)).astype(BF16)
    xb = SSD_INNER // 1024
    return pl.pallas_call(
        _ssd_kernel,
        grid=(2, nchunks),
        in_specs=[pl.BlockSpec((t, SSD_INNER), lambda d, c: (pos(d, c), 0)),
                  pl.BlockSpec((t, 1024), lambda d, c: (pos(d, c), xb)),
                  pl.BlockSpec((t, 1024), lambda d, c: (pos(d, c), xb + 1)),
                  pl.BlockSpec((1, t, SSD_HEADS), lambda d, c: (d, pos(d, c), 0)),
                  pl.BlockSpec((1, SSD_HEADS, t), lambda d, c: (d, 0, pos(d, c))),
                  pl.BlockSpec((1, 1, SSD_HEADS), lambda d, c: (d, 0, 0)),
                  pl.BlockSpec((1, SSD_HEADS, 1), lambda d, c: (d, 0, 0)),
                  pl.BlockSpec((1, 1, SSD_HEADS), lambda d, c: (d, 0, 0)),
                  pl.BlockSpec((1, SSD_HEADS, 1), lambda d, c: (d, 0, 0)),
                  pl.BlockSpec((1, t, t), lambda d, c: (d, 0, 0)),
                  pl.BlockSpec((SSD_HEADS, SSD_INNER), lambda d, c: (0, 0))],
        out_specs=pl.BlockSpec((1, t, SSD_INNER), lambda d, c: (d, pos(d, c), 0)),
        out_shape=jax.ShapeDtypeStruct((2, m, SSD_INNER), F32),
        scratch_shapes=[pltpu.VMEM((SSD_STATE, SSD_INNER), F32)],
        compiler_params=_cparams(("arbitrary", "arbitrary")),
        name="ssd_scan",
    )(xbc, xbc, xbc, dt, dtt, bias.reshape(2, 1, SSD_HEADS), bias.reshape(2, SSD_HEADS, 1),
      a_neg.reshape(2, 1, SSD_HEADS), a_neg.reshape(2, SSD_HEADS, 1), tri, expand)


def _ssd_finish_kernel(y_ref, x_ref, z_ref, d_ref, nw_ref, o_ref):
    z = z_ref[...]
    y = (x_ref[...] * d_ref[...] + y_ref[0] + y_ref[1]) * (z * jax.nn.sigmoid(z))
    o_ref[...] = _rms_rows(y, nw_ref[...]).astype(o_ref.dtype)


def _ssd_finish(y2, xbc, p, d_x, norm_w, *, tm):
    m = xbc.shape[0]
    w = SSD_INNER
    return pl.pallas_call(
        _ssd_finish_kernel,
        grid=(m // tm,),
        in_specs=[pl.BlockSpec((2, tm, w), lambda i: (0, i, 0)),
                  pl.BlockSpec((tm, w), lambda i: (i, 0)),
                  pl.BlockSpec((tm, w), lambda i: (i, 0)),
                  pl.BlockSpec((1, w), lambda i: (0, 0)),
                  pl.BlockSpec((1, w), lambda i: (0, 0))],
        out_specs=pl.BlockSpec((tm, w), lambda i: (i, 0)),
        out_shape=jax.ShapeDtypeStruct((m, w), BF16),
        compiler_params=_cparams(("arbitrary",)),
        name="ssd_finish",
    )(y2, xbc, p, d_x, norm_w.reshape(1, w))


def _even_mixer(hcat, g0, shift, scale, w_in, w_out, lam_p, subln, s5p, cos, sin, lam_init, *, n_ctx, tm):
    m = hcat.shape[0]
    qkv_w = 2 * DIFF_QK_W + DIFF_V_W
    p = _normmod_matmul(hcat, g0, shift, scale, w_in, n_ctx=n_ctx, tm=tm, tn=512, relu2=False,
                        out_dtype=F32, name="in_proj_qkv", cols=(0, qkv_w))
    q, k, vt = _rope_split(p, cos, sin, tm=_tile(m, 256, LANE))
    lp = lam_p.astype(F32)
    lam = (jnp.exp(jnp.sum(lp[0] * lp[1])) - jnp.exp(jnp.sum(lp[2] * lp[3])) + lam_init).reshape(1, 1)
    o_lat = _diff_attention(lam, q[n_ctx:], k, vt, subln, n_k=m, tq=_tile(m - n_ctx, 256, LANE),
                            tk=_tile(m, 768, LANE), lam_init=lam_init, name="diff_attn")
    o_ctx = _diff_attention(lam, q[:n_ctx], k, vt, subln, n_k=n_ctx, tq=_tile(n_ctx, 256, LANE),
                            tk=_tile(n_ctx, 768, LANE), lam_init=lam_init, name="diff_attn_ctx")
    x_t = _normmod_matmul(hcat, g0, shift, scale, w_in, n_ctx=n_ctx, tm=None, tn=512, relu2=False,
                          out_dtype=BF16, name="in_proj_s5", cols=(qkv_w, S5_WIDTH), slots=S5_T)
    x_t = x_t.reshape(S5_T, m // S5_T, S5_WIDTH)
    y_t = _s5_scan(x_t, *_s5_operators(*s5p[:8]), n_ctx_chunks=n_ctx // S5_T)
    s = _s5_glu(y_t, s5p[8], s5p[9])
    return jnp.concatenate([jnp.concatenate([o_ctx, o_lat], axis=0), s], axis=-1)


def _odd_mixer(hcat, g0, shift, scale, w_in, conv_w, conv_b, dt_bias, a_log, d_skip, norm_w, *, n_ctx, tm):
    p = _normmod_matmul(hcat, g0, shift, scale, w_in, n_ctx=n_ctx, tm=tm, tn=384, relu2=False,
                        out_dtype=F32, name="in_proj_odd")
    xbc = _conv_silu(p, conv_w, conv_b, n_ctx=n_ctx, col0=SSD_INNER)
    dt_raw = p[:, SSD_INNER + SSD_CONV_CH:]
    dt = jnp.stack([dt_raw[:, :SSD_HEADS], dt_raw[:, SSD_HEADS:]])
    dtt = jnp.transpose(dt, (0, 2, 1))
    a_neg = -jnp.exp(a_log.astype(F32))
    y2 = _ssd_scan(xbc, dt, dtt, dt_bias.astype(F32), a_neg, n_ctx=n_ctx)
    d_x = jnp.repeat(d_skip.astype(F32), SSD_HEAD_DIM).reshape(1, SSD_INNER)
    return _ssd_finish(y2, xbc, p, d_x, norm_w, tm=_tile(xbc.shape[0], 176, 16))


def kernel(x, c, ctx, c_ctx, w_mod, b_mod, norm_g, w_in_even, w_out_even, diff_lam, diff_subln, s5_lam_re, s5_lam_im, s5_log_dt, s5_b_re, s5_b_im, s5_c_re, s5_c_im, s5_d, s5_w_glu, s5_b_glu, w_in_odd, conv_w, conv_b, ssd_dt_bias, ssd_a_log, ssd_d, ssd_norm_w, w_out_odd, w_up, w_down):
    bsz, seq, d = x.shape
    assert bsz == 1 and d == D_MODEL
    n_ctx = ctx.shape[1]
    hcat = jnp.concatenate([ctx[0], x[0]], axis=0)
    m = hcat.shape[0]
    tm = _tile(m, 1056, 16)
    tm_acc = _tile(m, 768, 16)
    c_rows = jnp.zeros((8, d), F32).at[0].set(c_ctx).at[1].set(c[0])
    mods = _modulation(c_rows, w_mod, b_mod)[:, :2].reshape(DEPTH, 2, N_MOD, d)
    cos, sin = _rope_tables(n_ctx, seq)
    for i in range(DEPTH):
        md = mods[i]
        g = norm_g[i]
        j = i // 2
        if i % 2 == 0:
            lam_init = 0.8 - 0.6 * math.exp(-0.3 * i)
            s5p = (s5_lam_re[j], s5_lam_im[j], s5_log_dt[j], s5_b_re[j], s5_b_im[j], s5_c_re[j], s5_c_im[j],
                   s5_d[j], s5_w_glu[j], s5_b_glu[j])
            mix = _even_mixer(hcat, g[0], md[:, 0], md[:, 1], w_in_even[j], w_out_even[j], diff_lam[j],
                              diff_subln[j], s5p, cos, sin, lam_init, n_ctx=n_ctx, tm=tm)
            w_out = w_out_even[j]
        else:
            mix = _odd_mixer(hcat, g[0], md[:, 0], md[:, 1], w_in_odd[j], conv_w[j], conv_b[j], ssd_dt_bias[j],
                             ssd_a_log[j], ssd_d[j], ssd_norm_w[j], n_ctx=n_ctx, tm=tm)
            w_out = w_out_odd[j]
        hcat = _matmul_resid(mix, w_out, hcat, g[1], md[:, 2], n_ctx=n_ctx, tm=tm_acc, tk=512, name="out_proj")
        hid = _normmod_matmul(hcat, g[2], md[:, 3], md[:, 4], w_up[i], n_ctx=n_ctx, tm=tm, tn=512, relu2=True,
                              out_dtype=BF16, name="mlp_up")
        hcat = _matmul_resid(hid, w_down[i], hcat, g[3], md[:, 5], n_ctx=n_ctx, tm=tm_acc, tk=512, name="mlp_down")
    return hcat[n_ctx:][None]
```

```python
import functools
import math

import jax
import jax.numpy as jnp
from jax import lax
from jax.experimental import pallas as pl
from jax.experimental.pallas import tpu as pltpu

F32 = jnp.float32
BF16 = jnp.bfloat16

D_MODEL = 2048
DEPTH = 4
GRID_W = 64
N_MOD = 6
RMS_EPS = 1e-6

DIFF_HEADS = 8
DIFF_QK_DIM = 64
DIFF_V_DIM = 128
DIFF_QK_W = 1024
DIFF_V_W = 1024
ROPE_BASE = 10000.0
ROPE_AXIS_DIM = 32
LOG2_E = math.log2(math.e)
VT_ROWS = DIFF_V_DIM + 16

S5_WIDTH = 1024
S5_P = 16
S5_GROUPS = 64
S5_STATE = 64
S5_T = 16
S5_ROW = S5_T * S5_P
S5_GB = 128 // S5_P

SSD_INNER = 4096
SSD_HEAD_DIM = 64
SSD_HEADS = 64
SSD_GROUPS = 8
SSD_HPG = 8
SSD_STATE = 128
SSD_CONV = 5
SSD_CHUNK = 128
SSD_CONV_CH = 6144
ODD_IN_W = 10368

MLP_HIDDEN = 8192

LANE = 128
VMEM_LIMIT = 56 * 1024 * 1024


def _tile(n, target, mult):
    best = None
    for cand in range(mult, min(n, target) + 1, mult):
        if n % cand == 0:
            best = cand
    assert best is not None, (n, target, mult)
    return best


def _cparams(sem):
    return pltpu.CompilerParams(dimension_semantics=sem, vmem_limit_bytes=VMEM_LIMIT)


def _rms_rows(x, g):
    return x * lax.rsqrt(jnp.mean(x * x, axis=-1, keepdims=True) + RMS_EPS) * g


def _pick_rows(sel_ctx, ref):
    return jnp.where(sel_ctx, ref[0:1, :], ref[1:2, :])


ROW_CHUNK = 256


def _for_row_chunks(n_rows, row0, ctx_rows, fn):
    rc = _tile(n_rows, ROW_CHUNK, 16)

    def body(r, carry):
        start = pl.multiple_of(r * rc, rc)
        is_ctx = (row0 + start + lax.broadcasted_iota(jnp.int32, (rc, 1), 0)) < ctx_rows
        fn(pl.ds(start, rc), is_ctx)
        return carry

    lax.fori_loop(0, n_rows // rc, body, 0)


def _normmod_store(h_ref, g_ref, shift_ref, scale_ref, a_scr, *, row0, ctx_rows):
    def fn(rows, is_ctx):
        y = _rms_rows(h_ref[rows, :], g_ref[...])
        a_scr[rows, :] = (y * (1.0 + _pick_rows(is_ctx, scale_ref)) + _pick_rows(is_ctx, shift_ref)).astype(BF16)

    _for_row_chunks(h_ref.shape[0], row0, ctx_rows, fn)


def _gated_resid_store(o_ref, h_ref, g_ref, gate_ref, *, row0, ctx_rows):
    def fn(rows, is_ctx):
        o_ref[rows, :] = h_ref[rows, :] + _pick_rows(is_ctx, gate_ref) * _rms_rows(o_ref[rows, :], g_ref[...])

    _for_row_chunks(o_ref.shape[0], row0, ctx_rows, fn)


def _mod_kernel(a_ref, w_ref, b_ref, o_ref):
    a = a_ref[...]
    a = a * jax.nn.sigmoid(a)
    o_ref[0] = jnp.dot(a, w_ref[0], preferred_element_type=F32,
                       precision=lax.Precision.HIGHEST) + b_ref[0]


def _modulation(c_rows, w_mod, b_mod):
    depth, d, n = w_mod.shape
    tn = 1024
    return pl.pallas_call(
        _mod_kernel,
        grid=(depth, n // tn),
        in_specs=[pl.BlockSpec((8, d), lambda l, j: (0, 0)),
                  pl.BlockSpec((1, d, tn), lambda l, j: (l, 0, j)),
                  pl.BlockSpec((1, 1, tn), lambda l, j: (l, 0, j))],
        out_specs=pl.BlockSpec((1, 8, tn), lambda l, j: (l, 0, j)),
        out_shape=jax.ShapeDtypeStruct((depth, 8, n), F32),
        compiler_params=_cparams(("arbitrary", "arbitrary")),
        name="modulation",
    )(c_rows, w_mod, b_mod.reshape(depth, 1, n))


def _normmod_matmul_kernel(h_ref, g_ref, shift_ref, scale_ref, w_ref, o_ref, a_scr, *, ctx_rows, row_step):
    @pl.when(pl.program_id(1) == 0)
    def _():
        _normmod_store(h_ref, g_ref, shift_ref, scale_ref, a_scr,
                       row0=pl.program_id(0) * row_step, ctx_rows=ctx_rows)

    o_ref[...] = jnp.dot(a_scr[...], w_ref[...].astype(BF16), preferred_element_type=F32).astype(o_ref.dtype)


ACC_COLS = 512


def _accumulate(o_ref, lhs, w_ref, k):
    @pl.when(k == 0)
    def _():
        o_ref[...] = jnp.zeros(o_ref.shape, F32)

    for n0 in range(0, o_ref.shape[1], ACC_COLS):
        cols = slice(n0, n0 + ACC_COLS)
        o_ref[:, cols] += jnp.dot(lhs, w_ref[:, cols].astype(BF16), preferred_element_type=F32)


def _normmod_matmul(h, g, shift, scale, w, *, n_ctx, tm, tn, out_dtype, name, cols=None, slots=None):
    m, d = h.shape
    col0, n = (0, w.shape[1]) if cols is None else cols
    cb0 = col0 // tn
    assert col0 % tn == 0 and n % tn == 0
    if slots is None:
        steps, ctx_rows, row_step = m // tm, n_ctx, tm
        h_spec = pl.BlockSpec((tm, d), lambda i, j: (i, 0), pipeline_mode=pl.Buffered(1))
    else:
        tm = m // slots
        steps, ctx_rows, row_step = slots, n_ctx // slots, 0
        h = h.reshape(tm, slots * d)
        h_spec = pl.BlockSpec((tm, d), lambda i, j: (0, i))
    return pl.pallas_call(
        functools.partial(_normmod_matmul_kernel, ctx_rows=ctx_rows, row_step=row_step),
        grid=(steps, n // tn),
        in_specs=[h_spec,
                  pl.BlockSpec((1, d), lambda i, j: (0, 0)),
                  pl.BlockSpec((2, d), lambda i, j: (0, 0)),
                  pl.BlockSpec((2, d), lambda i, j: (0, 0)),
                  pl.BlockSpec((d, tn), lambda i, j: (0, cb0 + j))],
        out_specs=pl.BlockSpec((tm, tn), lambda i, j: (i, j)),
        out_shape=jax.ShapeDtypeStruct((m, n), out_dtype),
        scratch_shapes=[pltpu.VMEM((tm, d), BF16)],
        compiler_params=_cparams(("arbitrary", "arbitrary")),
        name=name,
    )(h, g.reshape(1, d), shift, scale, w)


def _matmul_resid_kernel(a_ref, w_ref, h_ref, g_ref, gate_ref, o_ref, *, n_ctx, tm, nk):
    k = pl.program_id(1)
    _accumulate(o_ref, a_ref[...], w_ref, k)

    @pl.when(k == nk - 1)
    def _():
        _gated_resid_store(o_ref, h_ref, g_ref, gate_ref, row0=pl.program_id(0) * tm, ctx_rows=n_ctx)


def _matmul_resid(a, w, h, g, gate, *, n_ctx, tm, tk, name):
    m, kdim = a.shape
    d = w.shape[1]
    nk = kdim // tk
    return pl.pallas_call(
        functools.partial(_matmul_resid_kernel, n_ctx=n_ctx, tm=tm, nk=nk),
        grid=(m // tm, nk),
        in_specs=[pl.BlockSpec((tm, tk), lambda i, k: (i, k)),
                  pl.BlockSpec((tk, d), lambda i, k: (k, 0)),
                  pl.BlockSpec((tm, d), lambda i, k: (i, 0)),
                  pl.BlockSpec((1, d), lambda i, k: (0, 0)),
                  pl.BlockSpec((2, d), lambda i, k: (0, 0))],
        out_specs=pl.BlockSpec((tm, d), lambda i, k: (i, 0)),
        out_shape=jax.ShapeDtypeStruct((m, d), F32),
        compiler_params=_cparams(("arbitrary", "arbitrary")),
        name=name,
    )(a, w, h, g.reshape(1, d), gate)


def _mlp_kernel(h_ref, g_in_ref, shift_ref, scale_ref, wup_ref, wdn_ref, g_out_ref, gate_ref, o_ref, a_scr,
                *, n_ctx, tm, nj):
    row0 = pl.program_id(0) * tm
    j = pl.program_id(1)

    @pl.when(j == 0)
    def _():
        _normmod_store(h_ref, g_in_ref, shift_ref, scale_ref, a_scr, row0=row0, ctx_rows=n_ctx)

    hid = jnp.dot(a_scr[...], wup_ref[...].astype(BF16), preferred_element_type=F32)
    hid = jnp.square(jnp.maximum(hid, 0.0)).astype(BF16)
    _accumulate(o_ref, hid, wdn_ref, j)

    @pl.when(j == nj - 1)
    def _():
        _gated_resid_store(o_ref, h_ref, g_out_ref, gate_ref, row0=row0, ctx_rows=n_ctx)


def _mlp_resid(h, g_in, shift, scale, w_up, w_down, g_out, gate, *, n_ctx, tm, tn):
    m, d = h.shape
    hidden = w_up.shape[1]
    nj = hidden // tn
    row = lambda i, j: (i, 0)
    fixed = lambda i, j: (0, 0)
    return pl.pallas_call(
        functools.partial(_mlp_kernel, n_ctx=n_ctx, tm=tm, nj=nj),
        grid=(m // tm, nj),
        in_specs=[pl.BlockSpec((tm, d), row, pipeline_mode=pl.Buffered(1)),
                  pl.BlockSpec((1, d), fixed),
                  pl.BlockSpec((2, d), fixed),
                  pl.BlockSpec((2, d), fixed),
                  pl.BlockSpec((d, tn), lambda i, j: (0, j)),
                  pl.BlockSpec((tn, d), lambda i, j: (j, 0)),
                  pl.BlockSpec((1, d), fixed),
                  pl.BlockSpec((2, d), fixed)],
        out_specs=pl.BlockSpec((tm, d), row, pipeline_mode=pl.Buffered(1)),
        out_shape=jax.ShapeDtypeStruct((m, d), F32),
        scratch_shapes=[pltpu.VMEM((tm, d), BF16)],
        compiler_params=_cparams(("arbitrary", "arbitrary")),
        name="mlp",
    )(h, g_in.reshape(1, d), shift, scale, w_up, w_down, g_out.reshape(1, d), gate)


def _rope_tables(n_ctx, seq_len):
    rows = seq_len // GRID_W
    row = jnp.repeat(jnp.arange(rows, dtype=F32), GRID_W)
    col = jnp.tile(jnp.arange(GRID_W, dtype=F32), rows)
    inv = ROPE_BASE ** (-jnp.arange(0, ROPE_AXIS_DIM, 2, dtype=F32) / ROPE_AXIS_DIM)
    ang_r = row[:, None] * inv
    ang_c = col[:, None] * inv
    ang = jnp.concatenate([ang_r, ang_r, ang_c, ang_c], axis=-1)
    cos = jnp.concatenate([jnp.ones((n_ctx, DIFF_QK_DIM), F32), jnp.cos(ang)], axis=0)
    sin = jnp.concatenate([jnp.zeros((n_ctx, DIFF_QK_DIM), F32), jnp.sin(ang)], axis=0)
    sign = jnp.where((jnp.arange(DIFF_QK_DIM) % 32) < 16, -1.0, 1.0).astype(F32)
    sin = sin * sign
    return jnp.tile(cos, (1, 2)), jnp.tile(sin, (1, 2))


def _rope_kernel(p_ref, cos_ref, sin_ref, q_ref, k_ref, vt_ref):
    cos = cos_ref[...]
    sin = sin_ref[...]
    low = (lax.broadcasted_iota(jnp.int32, (1, LANE), 1) % 32) < 16

    def rope(x):
        rot = jnp.where(low, pltpu.roll(x, LANE - 16, axis=1), pltpu.roll(x, 16, axis=1))
        return x * cos + rot * sin

    for c in range(DIFF_QK_W // LANE):
        sl = slice(c * LANE, (c + 1) * LANE)
        q_ref[:, sl] = (rope(p_ref[:, sl]) * (DIFF_QK_DIM ** -0.5 * LOG2_E)).astype(BF16)
        k_ref[:, sl] = rope(p_ref[:, DIFF_QK_W + c * LANE:DIFF_QK_W + (c + 1) * LANE]).astype(BF16)
    for h in range(DIFF_HEADS):
        v = p_ref[:, 2 * DIFF_QK_W + h * DIFF_V_DIM:2 * DIFF_QK_W + (h + 1) * DIFF_V_DIM]
        vt_ref[h, 0:DIFF_V_DIM, :] = v.T.astype(BF16)
        vt_ref[h, DIFF_V_DIM:VT_ROWS, :] = jnp.ones((VT_ROWS - DIFF_V_DIM, v.shape[0]), BF16)


def _rope_split(p, cos, sin, *, tm):
    m = p.shape[0]
    w = 2 * DIFF_QK_W + DIFF_V_W
    out = jax.ShapeDtypeStruct((m, DIFF_QK_W), BF16)
    return pl.pallas_call(
        _rope_kernel,
        grid=(m // tm,),
        in_specs=[pl.BlockSpec((tm, w), lambda i: (i, 0)),
                  pl.BlockSpec((tm, LANE), lambda i: (i, 0)),
                  pl.BlockSpec((tm, LANE), lambda i: (i, 0))],
        out_specs=[pl.BlockSpec((tm, DIFF_QK_W), lambda i: (i, 0)),
                   pl.BlockSpec((tm, DIFF_QK_W), lambda i: (i, 0)),
                   pl.BlockSpec((DIFF_HEADS, VT_ROWS, tm), lambda i: (0, 0, i))],
        out_shape=[out, out, jax.ShapeDtypeStruct((DIFF_HEADS, VT_ROWS, m), BF16)],
        compiler_params=_cparams(("arbitrary",)),
        name="rope_split",
    )(p, cos, sin)


def _attn_kernel(lam_ref, q0_ref, q1_ref, k0_ref, k1_ref, vt_ref, subln_ref, o_ref, acc_scr, s_scr,
                 *, tk, nk, out_scale):
    tq = q0_ref.shape[0]
    lam = lam_ref[0, 0]
    lane_head = lax.broadcasted_iota(jnp.int32, (1, LANE), 1) // DIFF_QK_DIM
    nt = (((1,), (1,)), ((), ()))
    k_refs = (k0_ref, k1_ref)

    def keys(kb):
        start = kb * tk
        return pl.ds(start if isinstance(kb, int) else pl.multiple_of(start, tk), tk)

    for sub in range(2):
        qs = [jnp.where(lane_head == sub, q_ref[...], jnp.zeros((), BF16)) for q_ref in (q0_ref, q1_ref)]
        acc_scr[...] = jnp.zeros(acc_scr.shape, F32)

        def scores(kb, slot):
            for mp in range(2):
                s_scr[slot, mp] = lax.dot_general(k_refs[mp][keys(kb), :], qs[mp], nt,
                                                  preferred_element_type=F32)

        def softmax_pv(kb, slot, ms):
            vt = vt_ref[sub, :, keys(kb)]
            new_ms = []
            for mp in range(2):
                s = s_scr[slot, mp]
                m_new = jnp.maximum(ms[mp], jnp.max(s, axis=0, keepdims=True))
                alpha = jnp.exp2(ms[mp] - m_new)
                p = jnp.exp2((s - m_new).astype(BF16))
                acc_scr[mp] = alpha * acc_scr[mp] + jnp.dot(vt, p, preferred_element_type=F32)
                new_ms.append(m_new)
            return tuple(new_ms)

        def pair(i, ms):
            scores(2 * i + 1, 1)
            ms = softmax_pv(2 * i, 0, ms)
            scores(2 * i + 2, 0)
            return softmax_pv(2 * i + 1, 1, ms)

        m_init = jnp.full((1, tq), -jnp.inf, F32)
        scores(0, 0)
        ms = lax.fori_loop(0, (nk - 1) // 2, pair, (m_init, m_init))
        if (nk - 1) % 2 == 1:
            scores(nk - 1, 1)
            ms = softmax_pv(nk - 2, 0, ms)
            softmax_pv(nk - 1, 1, ms)
        else:
            softmax_pv(nk - 1, 0, ms)
        o0 = acc_scr[0, 0:DIFF_V_DIM, :] / acc_scr[0, DIFF_V_DIM:DIFF_V_DIM + 1, :]
        o1 = acc_scr[1, 0:DIFF_V_DIM, :] / acc_scr[1, DIFF_V_DIM:DIFF_V_DIM + 1, :]
        o = o0 - lam * o1
        o = o * lax.rsqrt(jnp.mean(o * o, axis=0, keepdims=True) + RMS_EPS) * (subln_ref[...] * out_scale)
        o_ref[:, sub * DIFF_V_DIM:(sub + 1) * DIFF_V_DIM] = o.T.astype(o_ref.dtype)


def _diff_attention(lam, q, k, vt, subln, *, q_row0, n_q, n_k, tq, tk, lam_init, name):
    pairs = DIFF_HEADS // 2
    assert q_row0 % tq == 0 and n_q % tq == 0
    qb0 = q_row0 // tq
    nk = n_k // tk
    return pl.pallas_call(
        functools.partial(_attn_kernel, tk=tk, nk=nk, out_scale=1.0 - lam_init),
        grid=(pairs, n_q // tq),
        in_specs=[pl.BlockSpec(memory_space=pltpu.SMEM),
                  pl.BlockSpec((tq, LANE), lambda j, i: (qb0 + i, j)),
                  pl.BlockSpec((tq, LANE), lambda j, i: (qb0 + i, pairs + j)),
                  pl.BlockSpec((n_k, LANE), lambda j, i: (0, j)),
                  pl.BlockSpec((n_k, LANE), lambda j, i: (0, pairs + j)),
                  pl.BlockSpec((2, VT_ROWS, n_k), lambda j, i: (j, 0, 0)),
                  pl.BlockSpec((DIFF_V_DIM, 1), lambda j, i: (0, 0))],
        out_specs=pl.BlockSpec((tq, 2 * DIFF_V_DIM), lambda j, i: (i, j)),
        out_shape=jax.ShapeDtypeStruct((n_q, DIFF_V_W), BF16),
        scratch_shapes=[pltpu.VMEM((2, VT_ROWS, tq), F32), pltpu.VMEM((2, 2, tk, tq), F32)],
        compiler_params=_cparams(("arbitrary", "arbitrary")),
        name=name,
    )(lam, q, q, k, k, vt, subln.reshape(DIFF_V_DIM, 1).astype(F32))


def _s5_operators(lam_re, lam_im, log_dt, b_re, b_im, c_re, c_im, d_skip):
    t = S5_T
    steps = jnp.arange(t + 1, dtype=F32)
    ops = {}
    m_all = None
    for d in range(2):
        lam = lax.complex(lam_re[d].astype(F32), lam_im[d].astype(F32))
        dt = jnp.exp(log_dt[d].astype(F32))[:, None]
        lam_dt = lam * dt
        lam_bar = jnp.exp(lam_dt)
        b_bar = ((lam_bar - 1) / lam)[..., None] * lax.complex(b_re[d].astype(F32), b_im[d].astype(F32))
        c_mat = lax.complex(c_re[d].astype(F32), c_im[d].astype(F32))
        pw = jnp.exp(lam_dt[None] * steps[:, None, None])
        kern = jnp.real(jnp.einsum('gpn,tgn,gnq->tgpq', c_mat, pw[:t], b_bar,
                                   precision=lax.Precision.HIGHEST))
        s_idx = jnp.arange(t)[:, None]
        t_idx = jnp.arange(t)[None, :]
        lag = (t_idx - s_idx) if d == 0 else (s_idx - t_idx)
        onehot = (lag[None] == jnp.arange(t)[:, None, None]).astype(F32)
        m_d = jnp.einsum('xst,xgpq->gsqtp', onehot, kern,
                         precision=lax.Precision.HIGHEST).reshape(S5_GROUPS, S5_ROW, S5_ROW)
        m_all = m_d if m_all is None else m_all + m_d
        pw_in = pw[:t][::-1] if d == 0 else pw[:t]
        vin = pw_in[:, :, :, None] * b_bar[None]
        vin = jnp.transpose(vin, (1, 0, 3, 2)).reshape(S5_GROUPS, S5_ROW, S5_STATE)
        v_d = jnp.concatenate([jnp.real(vin), jnp.imag(vin)], axis=-1)
        pw_out = pw[1:] if d == 0 else pw[1:][::-1]
        wout = c_mat[None] * pw_out[:, :, None, :]
        wout = jnp.transpose(wout, (1, 3, 0, 2)).reshape(S5_GROUPS, S5_STATE, S5_ROW)
        w_d = jnp.concatenate([jnp.real(wout), -jnp.imag(wout)], axis=1)
        kk = (2.0 ** jnp.arange(10, dtype=F32)) * t
        ak = jnp.exp(lam_dt[:, None, :] * kk[None, :, None])
        a1 = jnp.concatenate([jnp.real(ak), jnp.real(ak)], axis=-1)
        a2 = jnp.concatenate([-jnp.imag(ak), jnp.imag(ak)], axis=-1)
        ops[d] = (v_d, w_d, a1, a2)
    eye = jnp.eye(S5_ROW, dtype=F32)
    dvec = jnp.tile(d_skip.astype(F32), (1, t))
    m_all = m_all + eye[None] * dvec[:, None, :]
    v_all = jnp.concatenate([ops[0][0], ops[1][0]], axis=-1)
    w_all = jnp.concatenate([ops[0][1], ops[1][1]], axis=1)
    pw_all = jnp.stack([ops[0][2], ops[0][3], ops[1][2], ops[1][3]], axis=1)
    return m_all.astype(BF16), v_all.astype(BF16), w_all.astype(BF16), pw_all


def _shift_rows(x, s, up):
    n = x.shape[0]
    if s >= n:
        return jnp.zeros_like(x)
    if s % 8 == 0:
        z = jnp.zeros((s, x.shape[1]), x.dtype)
        return jnp.concatenate([x[s:], z], axis=0) if up else jnp.concatenate([z, x[:n - s]], axis=0)
    row = lax.broadcasted_iota(jnp.int32, x.shape, 0)
    if up:
        return jnp.where(row < n - s, pltpu.roll(x, n - s, axis=0), 0.0)
    return jnp.where(row >= s, pltpu.roll(x, s, axis=0), 0.0)


def _s5_chunk_scan(z, a1_ref, a2_ref, up):
    x = _shift_rows(z, 1, up)
    n = x.shape[0]
    k = 0
    while (1 << k) < n:
        xs = _shift_rows(x, 1 << k, up)
        x = x + a1_ref[k:k + 1, :] * xs + a2_ref[k:k + 1, :] * pltpu.roll(xs, S5_STATE, axis=1)
        k += 1
    return x


def _s5_regroup_matrix():
    n = S5_T * LANE
    src = jnp.arange(n)
    t, g, q = src // LANE, (src % LANE) // S5_P, src % S5_P
    dst = g * S5_ROW + t * S5_P + q
    return (dst[:, None] == jnp.arange(n)[None, :]).astype(BF16)


def _s5_kernel(x_ref, perm_ref, m_ref, v_ref, w_ref, pw_ref, y_ref, *, n_ctx_chunks):
    nc = n_ctx_chunks
    perm = perm_ref[...]
    x = jnp.concatenate([x_ref[t] for t in range(S5_T)], axis=1)
    xp = jnp.dot(x, perm, preferred_element_type=F32).astype(BF16)
    ys = []
    for g in range(S5_GB):
        u = xp[:, g * S5_ROW:(g + 1) * S5_ROW]
        y = jnp.dot(u, m_ref[g], preferred_element_type=F32)
        z = jnp.dot(u, v_ref[g], preferred_element_type=F32)
        zf = z[:, :LANE]
        zb = z[:, LANE:]
        sf = _s5_chunk_scan(zf, pw_ref.at[g, 0], pw_ref.at[g, 1], up=False)
        zb = jnp.concatenate([zb[nc:], zb[:nc]], axis=0)
        sb = _s5_chunk_scan(zb, pw_ref.at[g, 2], pw_ref.at[g, 3], up=True)
        nl = sb.shape[0] - nc
        sb = jnp.concatenate([sb[nl:], sb[:nl]], axis=0)
        s = jnp.concatenate([sf, sb], axis=1).astype(BF16)
        ys.append(y + jnp.dot(s, w_ref[g], preferred_element_type=F32))
    y = jnp.concatenate(ys, axis=1)
    hi = y.astype(BF16)
    lo = (y - hi.astype(F32)).astype(BF16)
    nt = (((1,), (1,)), ((), ()))
    yt = (lax.dot_general(hi, perm, nt, preferred_element_type=F32)
          + lax.dot_general(lo, perm, nt, preferred_element_type=F32))
    for t in range(S5_T):
        y_ref[t] = yt[:, t * LANE:(t + 1) * LANE]


def _s5_scan(x_t, m_all, v_all, w_all, pw_all, *, n_ctx_chunks):
    t, rows, width = x_t.shape
    spec = pl.BlockSpec((S5_GB, S5_ROW, S5_ROW), lambda i: (i, 0, 0))
    n = S5_T * LANE
    return pl.pallas_call(
        functools.partial(_s5_kernel, n_ctx_chunks=n_ctx_chunks),
        grid=(width // LANE,),
        in_specs=[pl.BlockSpec((t, rows, LANE), lambda i: (0, 0, i)),
                  pl.BlockSpec((n, n), lambda i: (0, 0)), spec, spec, spec,
                  pl.BlockSpec((S5_GB, 4, 10, LANE), lambda i: (i, 0, 0, 0))],
        out_specs=pl.BlockSpec((t, rows, LANE), lambda i: (0, 0, i)),
        out_shape=jax.ShapeDtypeStruct((t, rows, width), F32),
        compiler_params=_cparams(("arbitrary",)),
        name="s5_scan",
    )(x_t, _s5_regroup_matrix(), m_all, v_all, w_all, pw_all)


def _glu_kernel(y_ref, w_ref, b_ref, o_ref):
    g = jax.nn.gelu(y_ref[0])
    gate = jnp.dot(g.astype(BF16), w_ref[...], preferred_element_type=F32) + b_ref[...]
    o_ref[...] = (g * jax.nn.sigmoid(gate)).astype(o_ref.dtype)


def _s5_glu(y_t, w_glu, b_glu):
    t, rows, w = y_t.shape
    out = pl.pallas_call(
        _glu_kernel,
        grid=(t,),
        in_specs=[pl.BlockSpec((1, rows, w), lambda i: (i, 0, 0)),
                  pl.BlockSpec((w, w), lambda i: (0, 0)),
                  pl.BlockSpec((1, w), lambda i: (0, 0))],
        out_specs=pl.BlockSpec((rows, w), lambda i: (0, i)),
        out_shape=jax.ShapeDtypeStruct((rows, t * w), BF16),
        compiler_params=_cparams(("arbitrary",)),
        name="s5_glu",
    )(y_t, w_glu.astype(BF16), b_glu.reshape(1, w).astype(F32))
    return out.reshape(rows * t, w)


CONV_HALO = 8


def _conv_kernel(x_ref, w_ref, b_ref, o_ref, pad_scr, *, n_ctx, tile):
    m = x_ref.shape[0]
    half = SSD_CONV // 2
    zeros = jnp.zeros((CONV_HALO, LANE), F32)
    segs = ((0, n_ctx, CONV_HALO), (n_ctx, m, 2 * CONV_HALO))
    pad_scr[0:CONV_HALO, :] = zeros
    pad_scr[CONV_HALO + n_ctx:2 * CONV_HALO + n_ctx, :] = zeros
    pad_scr[2 * CONV_HALO + m:3 * CONV_HALO + m, :] = zeros
    for lo, hi, off in segs:
        pad_scr[lo + off:hi + off, :] = x_ref[lo:hi, :]
    bias = b_ref[...]
    for lo, hi, off in segs:
        for r0 in range(lo, hi, tile):
            acc = bias
            for tap in range(SSD_CONV):
                acc = acc + w_ref[tap:tap + 1, :] * pad_scr[r0 + off + tap - half:r0 + off + tap - half + tile, :]
            o_ref[r0:r0 + tile, :] = acc * jax.nn.sigmoid(acc)


def _conv_silu(p, conv_w, conv_b, *, n_ctx, col0):
    m = p.shape[0]
    cb0 = col0 // LANE
    return pl.pallas_call(
        functools.partial(_conv_kernel, n_ctx=n_ctx, tile=256),
        grid=(SSD_CONV_CH // LANE,),
        in_specs=[pl.BlockSpec((m, LANE), lambda j: (0, cb0 + j)),
                  pl.BlockSpec((SSD_CONV, LANE), lambda j: (0, j)),
                  pl.BlockSpec((1, LANE), lambda j: (0, j))],
        out_specs=pl.BlockSpec((m, LANE), lambda j: (0, j)),
        out_shape=jax.ShapeDtypeStruct((m, SSD_CONV_CH), F32),
        scratch_shapes=[pltpu.VMEM((m + 3 * CONV_HALO, LANE), F32)],
        compiler_params=_cparams(("arbitrary",)),
        name="conv_silu",
    )(p, conv_w, conv_b.reshape(1, SSD_CONV_CH))


def _split_dot(x, e2):
    hi = x.astype(BF16)
    lo = (x - hi.astype(F32)).astype(BF16)
    return jnp.dot(jnp.concatenate([hi, lo], axis=1), e2, preferred_element_type=F32)


def _split3(x):
    x1 = x.astype(BF16)
    r1 = x - x1.astype(F32)
    x2 = r1.astype(BF16)
    x3 = (r1 - x2.astype(F32)).astype(BF16)
    return x1, x2, x3


def _ssd_kernel(x_ref, b_ref, c_ref, dt_ref, bias_ref, a_ref, tri_ref, e_ref, y_ref, state_scr):
    t = x_ref.shape[0]
    backward = pl.program_id(0) == 1

    @pl.when(pl.program_id(1) == 0)
    def _():
        state_scr[...] = jnp.zeros(state_scr.shape, F32)

    tri = tri_ref[0]
    expand = e_ref[...]
    dt = jax.nn.softplus(dt_ref[...] + bias_ref[...])
    a = dt * a_ref[...]
    dt = jnp.where(backward, pltpu.roll(dt, SSD_HEADS, axis=1), dt)
    a = jnp.where(backward, pltpu.roll(a, SSD_HEADS, axis=1), a)
    at = a.T
    cs = sum(jnp.dot(tri, ai, preferred_element_type=F32) for ai in _split3(a))
    cst = sum(lax.dot_general(ai, tri, (((1,), (1,)), ((), ())), preferred_element_type=F32)
              for ai in _split3(at))
    total = jnp.sum(a, axis=0, keepdims=True)
    dt_x = _split_dot(dt, expand)
    din_x = _split_dot(jnp.exp(total - cs), expand)
    dout_x = _split_dot(jnp.exp(cs), expand)
    cdec_x = _split_dot(jnp.broadcast_to(jnp.exp(total), (8, LANE)), expand)[0:1]

    xdt = x_ref[...] * dt_x
    xin = (xdt * din_x).astype(BF16)
    xdt = xdt.astype(BF16)
    lane_head = lax.broadcasted_iota(jnp.int32, (1, LANE), 1) // SSD_HEAD_DIM
    gw = SSD_HPG * SSD_HEAD_DIM
    for g in range(SSD_GROUPS):
        bf = b_ref[:, g * SSD_STATE:(g + 1) * SSD_STATE]
        bg = bf.astype(BF16)
        bgt = bf.T.astype(BF16)
        cg = c_ref[:, g * SSD_STATE:(g + 1) * SSD_STATE].astype(BF16)
        cb = lax.dot_general(cg, bg, (((1,), (1,)), ((), ())), preferred_element_type=F32)
        gs = slice(g * gw, (g + 1) * gw)
        st = state_scr[:, gs]
        y_off = jnp.dot(cg, st.astype(BF16), preferred_element_type=F32) * dout_x[:, gs]
        state_scr[:, gs] = st * cdec_x[:, gs] + jnp.dot(bgt, xin[:, gs], preferred_element_type=F32)
        for pr in range(SSD_HPG // 2):
            cols = slice(g * gw + pr * LANE, g * gw + (pr + 1) * LANE)
            xp = xdt[:, cols]
            mms, xhs = [], []
            for sub in range(2):
                h = g * SSD_HPG + pr * 2 + sub
                seg = jnp.exp(cs[:, h:h + 1] - cst[h:h + 1, :])
                mms.append(jnp.where(tri > 0, cb * seg, 0.0).astype(BF16))
                xhs.append(jnp.where(lane_head == sub, xp, jnp.zeros((), BF16)))
            yd = jnp.dot(jnp.concatenate(mms, axis=1), jnp.concatenate(xhs, axis=0),
                         preferred_element_type=F32)
            y_ref[0, :, cols] = yd + y_off[:, pr * LANE:(pr + 1) * LANE]


def _ssd_scan(xbc, p, bias, a_neg, *, n_ctx):
    m = xbc.shape[0]
    t = SSD_CHUNK
    nchunks = m // t
    ctx_chunks = n_ctx // t

    def pos(d, c):
        back = jnp.where(c < ctx_chunks, ctx_chunks - 1 - c, nchunks - 1 + ctx_chunks - c)
        return jnp.where(d == 0, c, back)

    idx = jnp.arange(t)
    tri = jnp.stack([idx[:, None] >= idx[None, :], idx[:, None] <= idx[None, :]]).astype(BF16)
    expand = (jnp.arange(LANE)[:, None] == (jnp.arange(SSD_INNER)[None, :] // SSD_HEAD_DIM)).astype(BF16)
    expand = jnp.concatenate([expand, expand], axis=0)
    xb = SSD_INNER // 1024
    dtb = (p.shape[1] - LANE) // LANE
    return pl.pallas_call(
        _ssd_kernel,
        grid=(2, nchunks),
        in_specs=[pl.BlockSpec((t, SSD_INNER), lambda d, c: (pos(d, c), 0)),
                  pl.BlockSpec((t, 1024), lambda d, c: (pos(d, c), xb)),
                  pl.BlockSpec((t, 1024), lambda d, c: (pos(d, c), xb + 1)),
                  pl.BlockSpec((t, LANE), lambda d, c: (pos(d, c), dtb)),
                  pl.BlockSpec((1, LANE), lambda d, c: (0, 0)),
                  pl.BlockSpec((1, LANE), lambda d, c: (0, 0)),
                  pl.BlockSpec((1, t, t), lambda d, c: (d, 0, 0)),
                  pl.BlockSpec((2 * LANE, SSD_INNER), lambda d, c: (0, 0))],
        out_specs=pl.BlockSpec((1, t, SSD_INNER), lambda d, c: (d, pos(d, c), 0)),
        out_shape=jax.ShapeDtypeStruct((2, m, SSD_INNER), F32),
        scratch_shapes=[pltpu.VMEM((SSD_STATE, SSD_INNER), F32)],
        compiler_params=_cparams(("arbitrary", "arbitrary")),
        name="ssd_scan",
    )(xbc, xbc, xbc, p, bias.reshape(1, LANE), a_neg.reshape(1, LANE), tri, expand)


def _ssd_finish_kernel(y_ref, x_ref, z_ref, d_ref, nw_ref, o_ref):
    z = z_ref[...]
    y = (x_ref[...] * d_ref[...] + y_ref[0] + y_ref[1]) * (z * jax.nn.sigmoid(z))
    o_ref[...] = _rms_rows(y, nw_ref[...]).astype(o_ref.dtype)


def _ssd_finish(y2, xbc, p, d_x, norm_w, *, tm):
    m = xbc.shape[0]
    w = SSD_INNER
    return pl.pallas_call(
        _ssd_finish_kernel,
        grid=(m // tm,),
        in_specs=[pl.BlockSpec((2, tm, w), lambda i: (0, i, 0)),
                  pl.BlockSpec((tm, w), lambda i: (i, 0)),
                  pl.BlockSpec((tm, w), lambda i: (i, 0)),
                  pl.BlockSpec((1, w), lambda i: (0, 0)),
                  pl.BlockSpec((1, w), lambda i: (0, 0))],
        out_specs=pl.BlockSpec((tm, w), lambda i: (i, 0)),
        out_shape=jax.ShapeDtypeStruct((m, w), BF16),
        compiler_params=_cparams(("arbitrary",)),
        name="ssd_finish",
    )(y2, xbc, p, d_x, norm_w.reshape(1, w))


def _even_mixer(hcat, g0, shift, scale, w_in, w_out, lam_p, subln, s5p, cos, sin, lam_init, *, n_ctx, tm):
    m = hcat.shape[0]
    qkv_w = 2 * DIFF_QK_W + DIFF_V_W
    p = _normmod_matmul(hcat, g0, shift, scale, w_in, n_ctx=n_ctx, tm=tm, tn=512,
                        out_dtype=F32, name="in_proj_qkv", cols=(0, qkv_w))
    q, k, vt = _rope_split(p, cos, sin, tm=_tile(m, 256, LANE))
    lp = lam_p.astype(F32)
    lam = (jnp.exp(jnp.sum(lp[0] * lp[1])) - jnp.exp(jnp.sum(lp[2] * lp[3])) + lam_init).reshape(1, 1)
    tq = _tile(math.gcd(n_ctx, m - n_ctx), 256, LANE)
    o_lat = _diff_attention(lam, q, k, vt, subln, q_row0=n_ctx, n_q=m - n_ctx, n_k=m, tq=tq,
                            tk=_tile(m, 768, LANE), lam_init=lam_init, name="diff_attn")
    o_ctx = _diff_attention(lam, q, k, vt, subln, q_row0=0, n_q=n_ctx, n_k=n_ctx, tq=tq,
                            tk=_tile(n_ctx, 768, LANE), lam_init=lam_init, name="diff_attn_ctx")
    x_t = _normmod_matmul(hcat, g0, shift, scale, w_in, n_ctx=n_ctx, tm=None, tn=512,
                          out_dtype=BF16, name="in_proj_s5", cols=(qkv_w, S5_WIDTH), slots=S5_T)
    x_t = x_t.reshape(S5_T, m // S5_T, S5_WIDTH)
    y_t = _s5_scan(x_t, *_s5_operators(*s5p[:8]), n_ctx_chunks=n_ctx // S5_T)
    s = _s5_glu(y_t, s5p[8], s5p[9])
    return jnp.concatenate([jnp.concatenate([o_ctx, o_lat], axis=0), s], axis=-1)


def _odd_mixer(hcat, g0, shift, scale, w_in, conv_w, conv_b, dt_bias, a_log, d_skip, norm_w, *, n_ctx, tm):
    p = _normmod_matmul(hcat, g0, shift, scale, w_in, n_ctx=n_ctx, tm=tm, tn=384,
                        out_dtype=F32, name="in_proj_odd")
    xbc = _conv_silu(p, conv_w, conv_b, n_ctx=n_ctx, col0=SSD_INNER)
    a_neg = -jnp.exp(a_log.astype(F32))
    y2 = _ssd_scan(xbc, p, dt_bias.astype(F32), a_neg, n_ctx=n_ctx)
    d_x = jnp.repeat(d_skip.astype(F32), SSD_HEAD_DIM).reshape(1, SSD_INNER)
    return _ssd_finish(y2, xbc, p, d_x, norm_w, tm=_tile(xbc.shape[0], 176, 16))


def kernel(x, c, ctx, c_ctx, w_mod, b_mod, norm_g, w_in_even, w_out_even, diff_lam, diff_subln, s5_lam_re, s5_lam_im, s5_log_dt, s5_b_re, s5_b_im, s5_c_re, s5_c_im, s5_d, s5_w_glu, s5_b_glu, w_in_odd, conv_w, conv_b, ssd_dt_bias, ssd_a_log, ssd_d, ssd_norm_w, w_out_odd, w_up, w_down):
    bsz, seq, d = x.shape
    assert bsz == 1 and d == D_MODEL
    n_ctx = ctx.shape[1]
    hcat = jnp.concatenate([ctx[0], x[0]], axis=0)
    m = hcat.shape[0]
    tm = _tile(m, 2112, 16)
    tm_acc = _tile(m, 1056, 16)
    c_rows = jnp.zeros((8, d), F32).at[0].set(c_ctx).at[1].set(c[0])
    mods = _modulation(c_rows, w_mod, b_mod)[:, :2].reshape(DEPTH, 2, N_MOD, d)
    cos, sin = _rope_tables(n_ctx, seq)
    for i in range(DEPTH):
        md = mods[i]
        g = norm_g[i]
        j = i // 2
        if i % 2 == 0:
            lam_init = 0.8 - 0.6 * math.exp(-0.3 * i)
            s5p = (s5_lam_re[j], s5_lam_im[j], s5_log_dt[j], s5_b_re[j], s5_b_im[j], s5_c_re[j], s5_c_im[j],
                   s5_d[j], s5_w_glu[j], s5_b_glu[j])
            mix = _even_mixer(hcat, g[0], md[:, 0], md[:, 1], w_in_even[j], w_out_even[j], diff_lam[j],
                              diff_subln[j], s5p, cos, sin, lam_init, n_ctx=n_ctx, tm=tm)
            w_out = w_out_even[j]
        else:
            mix = _odd_mixer(hcat, g[0], md[:, 0], md[:, 1], w_in_odd[j], conv_w[j], conv_b[j], ssd_dt_bias[j],
                             ssd_a_log[j], ssd_d[j], ssd_norm_w[j], n_ctx=n_ctx, tm=tm)
            w_out = w_out_odd[j]
        hcat = _matmul_resid(mix, w_out, hcat, g[1], md[:, 2], n_ctx=n_ctx, tm=tm_acc, tk=512, name="out_proj")
        hcat = _mlp_resid(hcat, g[2], md[:, 3], md[:, 4], w_up[i], w_down[i], g[3], md[:, 5],
                          n_ctx=n_ctx, tm=tm_acc, tn=512)
    return hcat[n_ctx:][None]
```

```python
import functools
import math

import jax
import jax.numpy as jnp
from jax import lax
from jax.experimental import pallas as pl
from jax.experimental.pallas import tpu as pltpu

F32 = jnp.float32
BF16 = jnp.bfloat16

D_MODEL = 2048
DEPTH = 4
GRID_W = 64
N_MOD = 6
RMS_EPS = 1e-6

DIFF_HEADS = 8
DIFF_QK_DIM = 64
DIFF_V_DIM = 128
DIFF_QK_W = 1024
DIFF_V_W = 1024
ROPE_BASE = 10000.0
ROPE_AXIS_DIM = 32
LOG2_E = math.log2(math.e)
VT_ROWS = DIFF_V_DIM + 16

S5_WIDTH = 1024
S5_P = 16
S5_GROUPS = 64
S5_STATE = 64
S5_T = 16
S5_ROW = S5_T * S5_P
S5_GB = 128 // S5_P

SSD_INNER = 4096
SSD_HEAD_DIM = 64
SSD_HEADS = 64
SSD_GROUPS = 8
SSD_HPG = 8
SSD_STATE = 128
SSD_CONV = 5
SSD_CHUNK = 128
SSD_CONV_CH = 6144
ODD_IN_W = 10368

MLP_HIDDEN = 8192

LANE = 128
VMEM_LIMIT = 56 * 1024 * 1024


def _tile(n, target, mult):
    best = None
    for cand in range(mult, min(n, target) + 1, mult):
        if n % cand == 0:
            best = cand
    assert best is not None, (n, target, mult)
    return best


def _cparams(sem):
    return pltpu.CompilerParams(dimension_semantics=sem, vmem_limit_bytes=VMEM_LIMIT)


def _rms_rows(x, g):
    return x * lax.rsqrt(jnp.mean(x * x, axis=-1, keepdims=True) + RMS_EPS) * g


def _pick_rows(sel_ctx, ref):
    return jnp.where(sel_ctx, ref[0:1, :], ref[1:2, :])


ROW_CHUNK = 256


def _for_row_chunks(n_rows, row0, ctx_rows, fn):
    rc = _tile(n_rows, ROW_CHUNK, 16)

    def body(r, carry):
        start = pl.multiple_of(r * rc, rc)
        is_ctx = (row0 + start + lax.broadcasted_iota(jnp.int32, (rc, 1), 0)) < ctx_rows
        fn(pl.ds(start, rc), is_ctx)
        return carry

    lax.fori_loop(0, n_rows // rc, body, 0)


def _normmod_store(h_ref, g_ref, shift_ref, scale_ref, a_scr, *, row0, ctx_rows):
    def fn(rows, is_ctx):
        y = _rms_rows(h_ref[rows, :], g_ref[...])
        a_scr[rows, :] = (y * (1.0 + _pick_rows(is_ctx, scale_ref)) + _pick_rows(is_ctx, shift_ref)).astype(BF16)

    _for_row_chunks(h_ref.shape[0], row0, ctx_rows, fn)


def _gated_resid_store(o_ref, h_ref, g_ref, gate_ref, *, row0, ctx_rows):
    def fn(rows, is_ctx):
        o_ref[rows, :] = h_ref[rows, :] + _pick_rows(is_ctx, gate_ref) * _rms_rows(o_ref[rows, :], g_ref[...])

    _for_row_chunks(o_ref.shape[0], row0, ctx_rows, fn)


def _mod_kernel(a_ref, w_ref, b_ref, o_ref):
    a = a_ref[...]
    a = a * jax.nn.sigmoid(a)
    o_ref[0] = jnp.dot(a, w_ref[0], preferred_element_type=F32,
                       precision=lax.Precision.HIGHEST) + b_ref[0]


def _modulation(c_rows, w_mod, b_mod):
    depth, d, n = w_mod.shape
    tn = 1024
    return pl.pallas_call(
        _mod_kernel,
        grid=(depth, n // tn),
        in_specs=[pl.BlockSpec((8, d), lambda l, j: (0, 0)),
                  pl.BlockSpec((1, d, tn), lambda l, j: (l, 0, j)),
                  pl.BlockSpec((1, 1, tn), lambda l, j: (l, 0, j))],
        out_specs=pl.BlockSpec((1, 8, tn), lambda l, j: (l, 0, j)),
        out_shape=jax.ShapeDtypeStruct((depth, 8, n), F32),
        compiler_params=_cparams(("arbitrary", "arbitrary")),
        name="modulation",
    )(c_rows, w_mod, b_mod.reshape(depth, 1, n))


def _normmod_matmul_kernel(h_ref, g_ref, shift_ref, scale_ref, w_ref, o_ref, a_scr, *, ctx_rows, row_step):
    @pl.when(pl.program_id(1) == 0)
    def _():
        _normmod_store(h_ref, g_ref, shift_ref, scale_ref, a_scr,
                       row0=pl.program_id(0) * row_step, ctx_rows=ctx_rows)

    o_ref[...] = jnp.dot(a_scr[...], w_ref[...].astype(BF16), preferred_element_type=F32).astype(o_ref.dtype)


ACC_COLS = 512


def _accumulate(o_ref, lhs, w_ref, k):
    @pl.when(k == 0)
    def _():
        o_ref[...] = jnp.zeros(o_ref.shape, F32)

    for n0 in range(0, o_ref.shape[1], ACC_COLS):
        cols = slice(n0, n0 + ACC_COLS)
        o_ref[:, cols] += jnp.dot(lhs, w_ref[:, cols].astype(BF16), preferred_element_type=F32)


def _normmod_matmul(h, g, shift, scale, w, layer, *, n_ctx, tm, tn, name):
    m, d = h.shape
    n = w.shape[2]
    return pl.pallas_call(
        functools.partial(_normmod_matmul_kernel, ctx_rows=n_ctx, row_step=tm),
        grid=(m // tm, n // tn),
        in_specs=[pl.BlockSpec((tm, d), lambda i, j: (i, 0), pipeline_mode=pl.Buffered(1)),
                  pl.BlockSpec((1, d), lambda i, j: (0, 0)),
                  pl.BlockSpec((2, d), lambda i, j: (0, 0)),
                  pl.BlockSpec((2, d), lambda i, j: (0, 0)),
                  pl.BlockSpec((None, d, tn), lambda i, j: (layer, 0, j))],
        out_specs=pl.BlockSpec((tm, tn), lambda i, j: (i, j)),
        out_shape=jax.ShapeDtypeStruct((m, n), F32),
        scratch_shapes=[pltpu.VMEM((tm, d), BF16)],
        compiler_params=_cparams(("arbitrary", "arbitrary")),
        name=name,
    )(h, g.reshape(1, d), shift, scale, w)


def _matmul_resid_kernel(a_ref, w_ref, h_ref, g_ref, gate_ref, o_ref, *, n_ctx, tm, nk):
    k = pl.program_id(1)
    _accumulate(o_ref, a_ref[...], w_ref, k)

    @pl.when(k == nk - 1)
    def _():
        _gated_resid_store(o_ref, h_ref, g_ref, gate_ref, row0=pl.program_id(0) * tm, ctx_rows=n_ctx)


def _matmul_resid(a, w, layer, h, g, gate, *, n_ctx, tm, tk, name):
    m, kdim = a.shape
    d = w.shape[2]
    nk = kdim // tk
    return pl.pallas_call(
        functools.partial(_matmul_resid_kernel, n_ctx=n_ctx, tm=tm, nk=nk),
        grid=(m // tm, nk),
        in_specs=[pl.BlockSpec((tm, tk), lambda i, k: (i, k)),
                  pl.BlockSpec((None, tk, d), lambda i, k: (layer, k, 0)),
                  pl.BlockSpec((tm, d), lambda i, k: (i, 0)),
                  pl.BlockSpec((1, d), lambda i, k: (0, 0)),
                  pl.BlockSpec((2, d), lambda i, k: (0, 0))],
        out_specs=pl.BlockSpec((tm, d), lambda i, k: (i, 0)),
        out_shape=jax.ShapeDtypeStruct((m, d), F32),
        compiler_params=_cparams(("arbitrary", "arbitrary")),
        name=name,
    )(a, w, h, g.reshape(1, d), gate)


def _mlp_kernel(h_ref, g_in_ref, shift_ref, scale_ref, wup_ref, wdn_ref, g_out_ref, gate_ref, o_ref, a_scr,
                *, n_ctx, tm, nj):
    row0 = pl.program_id(0) * tm
    j = pl.program_id(1)

    @pl.when(j == 0)
    def _():
        _normmod_store(h_ref, g_in_ref, shift_ref, scale_ref, a_scr, row0=row0, ctx_rows=n_ctx)

    hid = jnp.dot(a_scr[...], wup_ref[...].astype(BF16), preferred_element_type=F32)
    hid = jnp.square(jnp.maximum(hid, 0.0)).astype(BF16)
    _accumulate(o_ref, hid, wdn_ref, j)

    @pl.when(j == nj - 1)
    def _():
        _gated_resid_store(o_ref, h_ref, g_out_ref, gate_ref, row0=row0, ctx_rows=n_ctx)


def _mlp_resid(h, g_in, shift, scale, w_up, w_down, layer, g_out, gate, *, n_ctx, tm, tn):
    m, d = h.shape
    hidden = w_up.shape[2]
    nj = hidden // tn
    row = lambda i, j: (i, 0)
    fixed = lambda i, j: (0, 0)
    return pl.pallas_call(
        functools.partial(_mlp_kernel, n_ctx=n_ctx, tm=tm, nj=nj),
        grid=(m // tm, nj),
        in_specs=[pl.BlockSpec((tm, d), row, pipeline_mode=pl.Buffered(1)),
                  pl.BlockSpec((1, d), fixed),
                  pl.BlockSpec((2, d), fixed),
                  pl.BlockSpec((2, d), fixed),
                  pl.BlockSpec((None, d, tn), lambda i, j: (layer, 0, j)),
                  pl.BlockSpec((None, tn, d), lambda i, j: (layer, j, 0)),
                  pl.BlockSpec((1, d), fixed),
                  pl.BlockSpec((2, d), fixed)],
        out_specs=pl.BlockSpec((tm, d), row, pipeline_mode=pl.Buffered(1)),
        out_shape=jax.ShapeDtypeStruct((m, d), F32),
        scratch_shapes=[pltpu.VMEM((tm, d), BF16)],
        compiler_params=_cparams(("arbitrary", "arbitrary")),
        name="mlp",
    )(h, g_in.reshape(1, d), shift, scale, w_up, w_down, g_out.reshape(1, d), gate)


def _rope_tables(n_ctx, seq_len):
    rows = seq_len // GRID_W
    row = jnp.repeat(jnp.arange(rows, dtype=F32), GRID_W)
    col = jnp.tile(jnp.arange(GRID_W, dtype=F32), rows)
    inv = ROPE_BASE ** (-jnp.arange(0, ROPE_AXIS_DIM, 2, dtype=F32) / ROPE_AXIS_DIM)
    ang_r = row[:, None] * inv
    ang_c = col[:, None] * inv
    ang = jnp.concatenate([ang_r, ang_r, ang_c, ang_c], axis=-1)
    cos = jnp.concatenate([jnp.ones((n_ctx, DIFF_QK_DIM), F32), jnp.cos(ang)], axis=0)
    sin = jnp.concatenate([jnp.zeros((n_ctx, DIFF_QK_DIM), F32), jnp.sin(ang)], axis=0)
    sign = jnp.where((jnp.arange(DIFF_QK_DIM) % 32) < 16, -1.0, 1.0).astype(F32)
    sin = sin * sign
    return jnp.tile(cos, (1, 2)), jnp.tile(sin, (1, 2))


def _rope_kernel(p_ref, cos_ref, sin_ref, q_ref, k_ref, vt_ref):
    cos = cos_ref[...]
    sin = sin_ref[...]
    low = (lax.broadcasted_iota(jnp.int32, (1, LANE), 1) % 32) < 16

    def rope(x):
        rot = jnp.where(low, pltpu.roll(x, LANE - 16, axis=1), pltpu.roll(x, 16, axis=1))
        return x * cos + rot * sin

    for c in range(DIFF_QK_W // LANE):
        sl = slice(c * LANE, (c + 1) * LANE)
        q_ref[:, sl] = (rope(p_ref[:, sl]) * (DIFF_QK_DIM ** -0.5 * LOG2_E)).astype(BF16)
        k_ref[:, sl] = rope(p_ref[:, DIFF_QK_W + c * LANE:DIFF_QK_W + (c + 1) * LANE]).astype(BF16)
    for h in range(DIFF_HEADS):
        v = p_ref[:, 2 * DIFF_QK_W + h * DIFF_V_DIM:2 * DIFF_QK_W + (h + 1) * DIFF_V_DIM]
        vt_ref[h, 0:DIFF_V_DIM, :] = v.T.astype(BF16)
        vt_ref[h, DIFF_V_DIM:VT_ROWS, :] = jnp.ones((VT_ROWS - DIFF_V_DIM, v.shape[0]), BF16)


def _rope_split(p, cos, sin, *, tm):
    m = p.shape[0]
    w = 2 * DIFF_QK_W + DIFF_V_W
    out = jax.ShapeDtypeStruct((m, DIFF_QK_W), BF16)
    return pl.pallas_call(
        _rope_kernel,
        grid=(m // tm,),
        in_specs=[pl.BlockSpec((tm, w), lambda i: (i, 0)),
                  pl.BlockSpec((tm, LANE), lambda i: (i, 0)),
                  pl.BlockSpec((tm, LANE), lambda i: (i, 0))],
        out_specs=[pl.BlockSpec((tm, DIFF_QK_W), lambda i: (i, 0)),
                   pl.BlockSpec((tm, DIFF_QK_W), lambda i: (i, 0)),
                   pl.BlockSpec((DIFF_HEADS, VT_ROWS, tm), lambda i: (0, 0, i))],
        out_shape=[out, out, jax.ShapeDtypeStruct((DIFF_HEADS, VT_ROWS, m), BF16)],
        compiler_params=_cparams(("arbitrary",)),
        name="rope_split",
    )(p, cos, sin)


def _attn_kernel(lam_ref, q0_ref, q1_ref, k0_ref, k1_ref, vt_ref, subln_ref, o_ref, acc_scr, s_scr,
                 *, tk, nk, out_scale):
    tq = q0_ref.shape[0]
    lam = lam_ref[0, 0]
    lane_head = lax.broadcasted_iota(jnp.int32, (1, LANE), 1) // DIFF_QK_DIM
    nt = (((1,), (1,)), ((), ()))
    k_refs = (k0_ref, k1_ref)

    def keys(kb):
        start = kb * tk
        return pl.ds(start if isinstance(kb, int) else pl.multiple_of(start, tk), tk)

    for sub in range(2):
        qs = [jnp.where(lane_head == sub, q_ref[...], jnp.zeros((), BF16)) for q_ref in (q0_ref, q1_ref)]
        acc_scr[...] = jnp.zeros(acc_scr.shape, F32)

        def scores(kb, slot):
            for mp in range(2):
                s_scr[slot, mp] = lax.dot_general(k_refs[mp][keys(kb), :], qs[mp], nt,
                                                  preferred_element_type=F32)

        def softmax_pv(kb, slot, ms):
            vt = vt_ref[sub, :, keys(kb)]
            new_ms = []
            for mp in range(2):
                s = s_scr[slot, mp]
                m_new = jnp.maximum(ms[mp], jnp.max(s, axis=0, keepdims=True))
                alpha = jnp.exp2(ms[mp] - m_new)
                p = jnp.exp2((s - m_new).astype(BF16))
                acc_scr[mp] = alpha * acc_scr[mp] + jnp.dot(vt, p, preferred_element_type=F32)
                new_ms.append(m_new)
            return tuple(new_ms)

        def pair(i, ms):
            scores(2 * i + 1, 1)
            ms = softmax_pv(2 * i, 0, ms)
            scores(2 * i + 2, 0)
            return softmax_pv(2 * i + 1, 1, ms)

        m_init = jnp.full((1, tq), -jnp.inf, F32)
        scores(0, 0)
        ms = lax.fori_loop(0, (nk - 1) // 2, pair, (m_init, m_init))
        if (nk - 1) % 2 == 1:
            scores(nk - 1, 1)
            ms = softmax_pv(nk - 2, 0, ms)
            softmax_pv(nk - 1, 1, ms)
        else:
            softmax_pv(nk - 1, 0, ms)
        o0 = acc_scr[0, 0:DIFF_V_DIM, :] / acc_scr[0, DIFF_V_DIM:DIFF_V_DIM + 1, :]
        o1 = acc_scr[1, 0:DIFF_V_DIM, :] / acc_scr[1, DIFF_V_DIM:DIFF_V_DIM + 1, :]
        o = o0 - lam * o1
        o = o * lax.rsqrt(jnp.mean(o * o, axis=0, keepdims=True) + RMS_EPS) * (subln_ref[...] * out_scale)
        o_ref[:, sub * DIFF_V_DIM:(sub + 1) * DIFF_V_DIM] = o.T.astype(o_ref.dtype)


def _diff_attention(lam, q, k, vt, subln, *, q_row0, n_q, n_k, tq, tk, lam_init, name):
    pairs = DIFF_HEADS // 2
    assert q_row0 % tq == 0 and n_q % tq == 0
    qb0 = q_row0 // tq
    nk = n_k // tk
    return pl.pallas_call(
        functools.partial(_attn_kernel, tk=tk, nk=nk, out_scale=1.0 - lam_init),
        grid=(pairs, n_q // tq),
        in_specs=[pl.BlockSpec(memory_space=pltpu.SMEM),
                  pl.BlockSpec((tq, LANE), lambda j, i: (qb0 + i, j)),
                  pl.BlockSpec((tq, LANE), lambda j, i: (qb0 + i, pairs + j)),
                  pl.BlockSpec((n_k, LANE), lambda j, i: (0, j)),
                  pl.BlockSpec((n_k, LANE), lambda j, i: (0, pairs + j)),
                  pl.BlockSpec((2, VT_ROWS, n_k), lambda j, i: (j, 0, 0)),
                  pl.BlockSpec((DIFF_V_DIM, 1), lambda j, i: (0, 0))],
        out_specs=pl.BlockSpec((tq, 2 * DIFF_V_DIM), lambda j, i: (i, j)),
        out_shape=jax.ShapeDtypeStruct((n_q, DIFF_V_W), BF16),
        scratch_shapes=[pltpu.VMEM((2, VT_ROWS, tq), F32), pltpu.VMEM((2, 2, tk, tq), F32)],
        compiler_params=_cparams(("arbitrary", "arbitrary")),
        name=name,
    )(lam, q, q, k, k, vt, subln.reshape(DIFF_V_DIM, 1).astype(F32))


def _s5_operators(lam_re, lam_im, log_dt, b_re, b_im, c_re, c_im, d_skip):
    t = S5_T
    steps = jnp.arange(t + 1, dtype=F32)
    ops = {}
    m_all = None
    for d in range(2):
        lam = lax.complex(lam_re[d].astype(F32), lam_im[d].astype(F32))
        dt = jnp.exp(log_dt[d].astype(F32))[:, None]
        lam_dt = lam * dt
        lam_bar = jnp.exp(lam_dt)
        b_bar = ((lam_bar - 1) / lam)[..., None] * lax.complex(b_re[d].astype(F32), b_im[d].astype(F32))
        c_mat = lax.complex(c_re[d].astype(F32), c_im[d].astype(F32))
        pw = jnp.exp(lam_dt[None] * steps[:, None, None])
        kern = jnp.real(jnp.einsum('gpn,tgn,gnq->tgpq', c_mat, pw[:t], b_bar,
                                   precision=lax.Precision.HIGHEST))
        s_idx = jnp.arange(t)[:, None]
        t_idx = jnp.arange(t)[None, :]
        lag = (t_idx - s_idx) if d == 0 else (s_idx - t_idx)
        onehot = (lag[None] == jnp.arange(t)[:, None, None]).astype(F32)
        m_d = jnp.einsum('xst,xgpq->gsqtp', onehot, kern,
                         precision=lax.Precision.HIGHEST).reshape(S5_GROUPS, S5_ROW, S5_ROW)
        m_all = m_d if m_all is None else m_all + m_d
        pw_in = pw[:t][::-1] if d == 0 else pw[:t]
        vin = pw_in[:, :, :, None] * b_bar[None]
        vin = jnp.transpose(vin, (1, 0, 3, 2)).reshape(S5_GROUPS, S5_ROW, S5_STATE)
        v_d = jnp.concatenate([jnp.real(vin), jnp.imag(vin)], axis=-1)
        pw_out = pw[1:] if d == 0 else pw[1:][::-1]
        wout = c_mat[None] * pw_out[:, :, None, :]
        wout = jnp.transpose(wout, (1, 3, 0, 2)).reshape(S5_GROUPS, S5_STATE, S5_ROW)
        w_d = jnp.concatenate([jnp.real(wout), -jnp.imag(wout)], axis=1)
        kk = (2.0 ** jnp.arange(10, dtype=F32)) * t
        ak = jnp.exp(lam_dt[:, None, :] * kk[None, :, None])
        a1 = jnp.concatenate([jnp.real(ak), jnp.real(ak)], axis=-1)
        a2 = jnp.concatenate([-jnp.imag(ak), jnp.imag(ak)], axis=-1)
        ops[d] = (v_d, w_d, a1, a2)
    eye = jnp.eye(S5_ROW, dtype=F32)
    dvec = jnp.tile(d_skip.astype(F32), (1, t))
    m_all = m_all + eye[None] * dvec[:, None, :]
    v_all = jnp.concatenate([ops[0][0], ops[1][0]], axis=-1)
    w_all = jnp.concatenate([ops[0][1], ops[1][1]], axis=1)
    pw_all = jnp.stack([ops[0][2], ops[0][3], ops[1][2], ops[1][3]], axis=1)
    return m_all.astype(BF16), v_all.astype(BF16), w_all.astype(BF16), pw_all


def _shift_rows(x, s, up):
    n = x.shape[0]
    if s >= n:
        return jnp.zeros_like(x)
    if s % 8 == 0:
        z = jnp.zeros((s, x.shape[1]), x.dtype)
        return jnp.concatenate([x[s:], z], axis=0) if up else jnp.concatenate([z, x[:n - s]], axis=0)
    row = lax.broadcasted_iota(jnp.int32, x.shape, 0)
    if up:
        return jnp.where(row < n - s, pltpu.roll(x, n - s, axis=0), 0.0)
    return jnp.where(row >= s, pltpu.roll(x, s, axis=0), 0.0)


def _s5_chunk_scan(z, a1_ref, a2_ref, up):
    x = _shift_rows(z, 1, up)
    n = x.shape[0]
    k = 0
    while (1 << k) < n:
        xs = _shift_rows(x, 1 << k, up)
        x = x + a1_ref[k:k + 1, :] * xs + a2_ref[k:k + 1, :] * pltpu.roll(xs, S5_STATE, axis=1)
        k += 1
    return x


def _s5_regroup_matrix():
    n = S5_T * LANE
    src = jnp.arange(n)
    t, g, q = src // LANE, (src % LANE) // S5_P, src % S5_P
    dst = g * S5_ROW + t * S5_P + q
    return (dst[:, None] == jnp.arange(n)[None, :]).astype(BF16)


def _s5_kernel(x_ref, perm_ref, m_ref, v_ref, w_ref, pw_ref, y_ref, *, n_ctx_chunks):
    nc = n_ctx_chunks
    rows = x_ref.shape[0] // S5_T
    perm = perm_ref[...]
    x = jnp.concatenate([x_ref[pl.ds(t, rows, stride=S5_T), :].astype(BF16) for t in range(S5_T)],
                        axis=1)
    xp = jnp.dot(x, perm, preferred_element_type=F32).astype(BF16)
    ys = []
    for g in range(S5_GB):
        u = xp[:, g * S5_ROW:(g + 1) * S5_ROW]
        y = jnp.dot(u, m_ref[g], preferred_element_type=F32)
        z = jnp.dot(u, v_ref[g], preferred_element_type=F32)
        zf = z[:, :LANE]
        zb = z[:, LANE:]
        sf = _s5_chunk_scan(zf, pw_ref.at[g, 0], pw_ref.at[g, 1], up=False)
        zb = jnp.concatenate([zb[nc:], zb[:nc]], axis=0)
        sb = _s5_chunk_scan(zb, pw_ref.at[g, 2], pw_ref.at[g, 3], up=True)
        nl = sb.shape[0] - nc
        sb = jnp.concatenate([sb[nl:], sb[:nl]], axis=0)
        s = jnp.concatenate([sf, sb], axis=1).astype(BF16)
        ys.append(y + jnp.dot(s, w_ref[g], preferred_element_type=F32))
    y = jnp.concatenate(ys, axis=1)
    hi = y.astype(BF16)
    lo = (y - hi.astype(F32)).astype(BF16)
    nt = (((1,), (1,)), ((), ()))
    yt = (lax.dot_general(hi, perm, nt, preferred_element_type=F32)
          + lax.dot_general(lo, perm, nt, preferred_element_type=F32))
    for t in range(S5_T):
        y_ref[pl.ds(t, rows, stride=S5_T), :] = yt[:, t * LANE:(t + 1) * LANE]


def _s5_scan(p, col0, m_all, v_all, w_all, pw_all, *, n_ctx_chunks):
    m = p.shape[0]
    cb0 = col0 // LANE
    spec = pl.BlockSpec((S5_GB, S5_ROW, S5_ROW), lambda i: (i, 0, 0))
    n = S5_T * LANE
    return pl.pallas_call(
        functools.partial(_s5_kernel, n_ctx_chunks=n_ctx_chunks),
        grid=(S5_WIDTH // LANE,),
        in_specs=[pl.BlockSpec((m, LANE), lambda i: (0, cb0 + i)),
                  pl.BlockSpec((n, n), lambda i: (0, 0)), spec, spec, spec,
                  pl.BlockSpec((S5_GB, 4, 10, LANE), lambda i: (i, 0, 0, 0))],
        out_specs=pl.BlockSpec((m, LANE), lambda i: (0, i)),
        out_shape=jax.ShapeDtypeStruct((m, S5_WIDTH), F32),
        compiler_params=_cparams(("arbitrary",)),
        name="s5_scan",
    )(p, _s5_regroup_matrix(), m_all, v_all, w_all, pw_all)


def _glu_kernel(y_ref, w_ref, b_ref, o_ref):
    g = jax.nn.gelu(y_ref[...])
    gate = jnp.dot(g.astype(BF16), w_ref[...], preferred_element_type=F32) + b_ref[...]
    o_ref[...] = (g * jax.nn.sigmoid(gate)).astype(o_ref.dtype)


def _s5_glu(y, w_glu, b_glu, *, tm):
    m, w = y.shape
    return pl.pallas_call(
        _glu_kernel,
        grid=(m // tm,),
        in_specs=[pl.BlockSpec((tm, w), lambda i: (i, 0)),
                  pl.BlockSpec((w, w), lambda i: (0, 0)),
                  pl.BlockSpec((1, w), lambda i: (0, 0))],
        out_specs=pl.BlockSpec((tm, w), lambda i: (i, 0)),
        out_shape=jax.ShapeDtypeStruct((m, w), BF16),
        compiler_params=_cparams(("arbitrary",)),
        name="s5_glu",
    )(y, w_glu.astype(BF16), b_glu.reshape(1, w).astype(F32))


CONV_HALO = 8


def _conv_kernel(x_ref, w_ref, b_ref, o_ref, pad_scr, *, n_ctx, tile):
    m = x_ref.shape[0]
    half = SSD_CONV // 2
    zeros = jnp.zeros((CONV_HALO, LANE), F32)
    segs = ((0, n_ctx, CONV_HALO), (n_ctx, m, 2 * CONV_HALO))
    pad_scr[0:CONV_HALO, :] = zeros
    pad_scr[CONV_HALO + n_ctx:2 * CONV_HALO + n_ctx, :] = zeros
    pad_scr[2 * CONV_HALO + m:3 * CONV_HALO + m, :] = zeros
    for lo, hi, off in segs:
        pad_scr[lo + off:hi + off, :] = x_ref[lo:hi, :]
    bias = b_ref[...]
    for lo, hi, off in segs:
        for r0 in range(lo, hi, tile):
            acc = bias
            for tap in range(SSD_CONV):
                acc = acc + w_ref[tap:tap + 1, :] * pad_scr[r0 + off + tap - half:r0 + off + tap - half + tile, :]
            o_ref[r0:r0 + tile, :] = acc * jax.nn.sigmoid(acc)


def _conv_silu(p, conv_w, conv_b, *, n_ctx, col0):
    m = p.shape[0]
    cb0 = col0 // LANE
    return pl.pallas_call(
        functools.partial(_conv_kernel, n_ctx=n_ctx, tile=256),
        grid=(SSD_CONV_CH // LANE,),
        in_specs=[pl.BlockSpec((m, LANE), lambda j: (0, cb0 + j)),
                  pl.BlockSpec((SSD_CONV, LANE), lambda j: (0, j)),
                  pl.BlockSpec((1, LANE), lambda j: (0, j))],
        out_specs=pl.BlockSpec((m, LANE), lambda j: (0, j)),
        out_shape=jax.ShapeDtypeStruct((m, SSD_CONV_CH), F32),
        scratch_shapes=[pltpu.VMEM((m + 3 * CONV_HALO, LANE), F32)],
        compiler_params=_cparams(("arbitrary",)),
        name="conv_silu",
    )(p, conv_w, conv_b.reshape(1, SSD_CONV_CH))


def _split_dot(x, e2):
    hi = x.astype(BF16)
    lo = (x - hi.astype(F32)).astype(BF16)
    return jnp.dot(jnp.concatenate([hi, lo], axis=1), e2, preferred_element_type=F32)


def _split3(x):
    x1 = x.astype(BF16)
    r1 = x - x1.astype(F32)
    x2 = r1.astype(BF16)
    x3 = (r1 - x2.astype(F32)).astype(BF16)
    return x1, x2, x3


def _ssd_kernel(x_ref, b_ref, c_ref, dt_ref, bias_ref, a_ref, tri_ref, e_ref, y_ref, state_scr):
    t = x_ref.shape[0]
    backward = pl.program_id(0) == 1

    @pl.when(pl.program_id(1) == 0)
    def _():
        state_scr[...] = jnp.zeros(state_scr.shape, F32)

    tri = tri_ref[0]
    expand = e_ref[...]
    dt = jax.nn.softplus(dt_ref[...] + bias_ref[...])
    a = dt * a_ref[...]
    dt = jnp.where(backward, pltpu.roll(dt, SSD_HEADS, axis=1), dt)
    a = jnp.where(backward, pltpu.roll(a, SSD_HEADS, axis=1), a)
    at = a.T
    cs = sum(jnp.dot(tri, ai, preferred_element_type=F32) for ai in _split3(a))
    cst = sum(lax.dot_general(ai, tri, (((1,), (1,)), ((), ())), preferred_element_type=F32)
              for ai in _split3(at))
    total = jnp.sum(a, axis=0, keepdims=True)
    dt_x = _split_dot(dt, expand)
    din_x = _split_dot(jnp.exp(total - cs), expand)
    dout_x = _split_dot(jnp.exp(cs), expand)
    cdec_x = _split_dot(jnp.broadcast_to(jnp.exp(total), (8, LANE)), expand)[0:1]

    xdt = x_ref[...] * dt_x
    xin = (xdt * din_x).astype(BF16)
    xdt = xdt.astype(BF16)
    lane_head = lax.broadcasted_iota(jnp.int32, (1, LANE), 1) // SSD_HEAD_DIM
    gw = SSD_HPG * SSD_HEAD_DIM
    for g in range(SSD_GROUPS):
        bf = b_ref[:, g * SSD_STATE:(g + 1) * SSD_STATE]
        bg = bf.astype(BF16)
        bgt = bf.T.astype(BF16)
        cg = c_ref[:, g * SSD_STATE:(g + 1) * SSD_STATE].astype(BF16)
        cb = lax.dot_general(cg, bg, (((1,), (1,)), ((), ())), preferred_element_type=F32)
        gs = slice(g * gw, (g + 1) * gw)
        st = state_scr[:, gs]
        y_off = jnp.dot(cg, st.astype(BF16), preferred_element_type=F32) * dout_x[:, gs]
        state_scr[:, gs] = st * cdec_x[:, gs] + jnp.dot(bgt, xin[:, gs], preferred_element_type=F32)
        for pr in range(SSD_HPG // 2):
            cols = slice(g * gw + pr * LANE, g * gw + (pr + 1) * LANE)
            xp = xdt[:, cols]
            mms, xhs = [], []
            for sub in range(2):
                h = g * SSD_HPG + pr * 2 + sub
                seg = jnp.exp(cs[:, h:h + 1] - cst[h:h + 1, :])
                mms.append(jnp.where(tri > 0, cb * seg, 0.0).astype(BF16))
                xhs.append(jnp.where(lane_head == sub, xp, jnp.zeros((), BF16)))
            yd = jnp.dot(jnp.concatenate(mms, axis=1), jnp.concatenate(xhs, axis=0),
                         preferred_element_type=F32)
            y_ref[0, :, cols] = yd + y_off[:, pr * LANE:(pr + 1) * LANE]


def _ssd_scan(xbc, p, bias, a_neg, *, n_ctx):
    m = xbc.shape[0]
    t = SSD_CHUNK
    nchunks = m // t
    ctx_chunks = n_ctx // t

    def pos(d, c):
        back = jnp.where(c < ctx_chunks, ctx_chunks - 1 - c, nchunks - 1 + ctx_chunks - c)
        return jnp.where(d == 0, c, back)

    idx = jnp.arange(t)
    tri = jnp.stack([idx[:, None] >= idx[None, :], idx[:, None] <= idx[None, :]]).astype(BF16)
    expand = (jnp.arange(LANE)[:, None] == (jnp.arange(SSD_INNER)[None, :] // SSD_HEAD_DIM)).astype(BF16)
    expand = jnp.concatenate([expand, expand], axis=0)
    xb = SSD_INNER // 1024
    dtb = (p.shape[1] - LANE) // LANE
    return pl.pallas_call(
        _ssd_kernel,
        grid=(2, nchunks),
        in_specs=[pl.BlockSpec((t, SSD_INNER), lambda d, c: (pos(d, c), 0)),
                  pl.BlockSpec((t, 1024), lambda d, c: (pos(d, c), xb)),
                  pl.BlockSpec((t, 1024), lambda d, c: (pos(d, c), xb + 1)),
                  pl.BlockSpec((t, LANE), lambda d, c: (pos(d, c), dtb)),
                  pl.BlockSpec((1, LANE), lambda d, c: (0, 0)),
                  pl.BlockSpec((1, LANE), lambda d, c: (0, 0)),
                  pl.BlockSpec((1, t, t), lambda d, c: (d, 0, 0)),
                  pl.BlockSpec((2 * LANE, SSD_INNER), lambda d, c: (0, 0))],
        out_specs=pl.BlockSpec((1, t, SSD_INNER), lambda d, c: (d, pos(d, c), 0)),
        out_shape=jax.ShapeDtypeStruct((2, m, SSD_INNER), F32),
        scratch_shapes=[pltpu.VMEM((SSD_STATE, SSD_INNER), F32)],
        compiler_params=_cparams(("arbitrary", "arbitrary")),
        name="ssd_scan",
    )(xbc, xbc, xbc, p, bias.reshape(1, LANE), a_neg.reshape(1, LANE), tri, expand)


def _ssd_finish_kernel(y_ref, x_ref, z_ref, d_ref, nw_ref, o_ref):
    z = z_ref[...]
    y = (x_ref[...] * d_ref[...] + y_ref[0] + y_ref[1]) * (z * jax.nn.sigmoid(z))
    o_ref[...] = _rms_rows(y, nw_ref[...]).astype(o_ref.dtype)


def _ssd_finish(y2, xbc, p, d_x, norm_w, *, tm):
    m = xbc.shape[0]
    w = SSD_INNER
    return pl.pallas_call(
        _ssd_finish_kernel,
        grid=(m // tm,),
        in_specs=[pl.BlockSpec((2, tm, w), lambda i: (0, i, 0)),
                  pl.BlockSpec((tm, w), lambda i: (i, 0)),
                  pl.BlockSpec((tm, w), lambda i: (i, 0)),
                  pl.BlockSpec((1, w), lambda i: (0, 0)),
                  pl.BlockSpec((1, w), lambda i: (0, 0))],
        out_specs=pl.BlockSpec((tm, w), lambda i: (i, 0)),
        out_shape=jax.ShapeDtypeStruct((m, w), BF16),
        compiler_params=_cparams(("arbitrary",)),
        name="ssd_finish",
    )(y2, xbc, p, d_x, norm_w.reshape(1, w))


def _even_mixer(hcat, g0, shift, scale, w_in, j, lam_p, subln, s5p, cos, sin, lam_init, *, n_ctx, tm):
    m = hcat.shape[0]
    p = _normmod_matmul(hcat, g0, shift, scale, w_in, j, n_ctx=n_ctx, tm=tm, tn=512,
                        name="in_proj_even")
    q, k, vt = _rope_split(p, cos, sin, tm=_tile(m, 256, LANE))
    lp = lam_p.astype(F32)
    lam = (jnp.exp(jnp.sum(lp[0] * lp[1])) - jnp.exp(jnp.sum(lp[2] * lp[3])) + lam_init).reshape(1, 1)
    tq = _tile(math.gcd(n_ctx, m - n_ctx), 256, LANE)
    o_lat = _diff_attention(lam, q, k, vt, subln, q_row0=n_ctx, n_q=m - n_ctx, n_k=m, tq=tq,
                            tk=_tile(m, 768, LANE), lam_init=lam_init, name="diff_attn")
    o_ctx = _diff_attention(lam, q, k, vt, subln, q_row0=0, n_q=n_ctx, n_k=n_ctx, tq=tq,
                            tk=_tile(n_ctx, 768, LANE), lam_init=lam_init, name="diff_attn_ctx")
    y = _s5_scan(p, 2 * DIFF_QK_W + DIFF_V_W, *_s5_operators(*s5p[:8]), n_ctx_chunks=n_ctx // S5_T)
    s = _s5_glu(y, s5p[8], s5p[9], tm=_tile(m, 1056, 16))
    return jnp.concatenate([jnp.concatenate([o_ctx, o_lat], axis=0), s], axis=-1)


def _odd_mixer(hcat, g0, shift, scale, w_in, j, conv_w, conv_b, dt_bias, a_log, d_skip, norm_w, *, n_ctx, tm):
    p = _normmod_matmul(hcat, g0, shift, scale, w_in, j, n_ctx=n_ctx, tm=tm, tn=384,
                        name="in_proj_odd")
    xbc = _conv_silu(p, conv_w, conv_b, n_ctx=n_ctx, col0=SSD_INNER)
    a_neg = -jnp.exp(a_log.astype(F32))
    y2 = _ssd_scan(xbc, p, dt_bias.astype(F32), a_neg, n_ctx=n_ctx)
    d_x = jnp.repeat(d_skip.astype(F32), SSD_HEAD_DIM).reshape(1, SSD_INNER)
    return _ssd_finish(y2, xbc, p, d_x, norm_w, tm=_tile(xbc.shape[0], 176, 16))


def kernel(x, c, ctx, c_ctx, w_mod, b_mod, norm_g, w_in_even, w_out_even, diff_lam, diff_subln, s5_lam_re, s5_lam_im, s5_log_dt, s5_b_re, s5_b_im, s5_c_re, s5_c_im, s5_d, s5_w_glu, s5_b_glu, w_in_odd, conv_w, conv_b, ssd_dt_bias, ssd_a_log, ssd_d, ssd_norm_w, w_out_odd, w_up, w_down):
    bsz, seq, d = x.shape
    assert bsz == 1 and d == D_MODEL
    n_ctx = ctx.shape[1]
    hcat = jnp.concatenate([ctx[0], x[0]], axis=0)
    m = hcat.shape[0]
    tm = _tile(m, 2112, 16)
    tm_acc = _tile(m, 1056, 16)
    c_rows = jnp.zeros((8, d), F32).at[0].set(c_ctx).at[1].set(c[0])
    mods = _modulation(c_rows, w_mod, b_mod)[:, :2].reshape(DEPTH, 2, N_MOD, d)
    cos, sin = _rope_tables(n_ctx, seq)
    for i in range(DEPTH):
        md = mods[i]
        g = norm_g[i]
        j = i // 2
        if i % 2 == 0:
            lam_init = 0.8 - 0.6 * math.exp(-0.3 * i)
            s5p = (s5_lam_re[j], s5_lam_im[j], s5_log_dt[j], s5_b_re[j], s5_b_im[j], s5_c_re[j], s5_c_im[j],
                   s5_d[j], s5_w_glu[j], s5_b_glu[j])
            mix = _even_mixer(hcat, g[0], md[:, 0], md[:, 1], w_in_even, j, diff_lam[j],
                              diff_subln[j], s5p, cos, sin, lam_init, n_ctx=n_ctx, tm=tm)
            w_out = w_out_even
        else:
            mix = _odd_mixer(hcat, g[0], md[:, 0], md[:, 1], w_in_odd, j, conv_w[j], conv_b[j], ssd_dt_bias[j],
                             ssd_a_log[j], ssd_d[j], ssd_norm_w[j], n_ctx=n_ctx, tm=tm)
            w_out = w_out_odd
        hcat = _matmul_resid(mix, w_out, j, hcat, g[1], md[:, 2], n_ctx=n_ctx, tm=tm_acc, tk=512, name="out_proj")
        hcat = _mlp_resid(hcat, g[2], md[:, 3], md[:, 4], w_up, w_down, i, g[3], md[:, 5],
                          n_ctx=n_ctx, tm=tm_acc, tn=512)
    return hcat[n_ctx:][None]
```

```python
import functools
import math

import jax
import jax.numpy as jnp
from jax import lax
from jax.experimental import pallas as pl
from jax.experimental.pallas import tpu as pltpu

F32 = jnp.float32
BF16 = jnp.bfloat16

D_MODEL = 2048
DEPTH = 4
GRID_W = 64
N_MOD = 6
RMS_EPS = 1e-6

DIFF_HEADS = 8
DIFF_QK_DIM = 64
DIFF_V_DIM = 128
DIFF_QK_W = 1024
DIFF_V_W = 1024
ROPE_BASE = 10000.0
ROPE_AXIS_DIM = 32
LOG2_E = math.log2(math.e)
VT_ROWS = DIFF_V_DIM + 16

S5_WIDTH = 1024
S5_P = 16
S5_GROUPS = 64
S5_STATE = 64
S5_T = 16
S5_ROW = S5_T * S5_P
S5_GB = 128 // S5_P

SSD_INNER = 4096
SSD_HEAD_DIM = 64
SSD_HEADS = 64
SSD_GROUPS = 8
SSD_HPG = 8
SSD_STATE = 128
SSD_CONV = 5
SSD_CHUNK = 128
SSD_CONV_CH = 6144
ODD_IN_W = 10368

MLP_HIDDEN = 8192

LANE = 128
VMEM_LIMIT = 56 * 1024 * 1024


def _tile(n, target, mult):
    best = None
    for cand in range(mult, min(n, target) + 1, mult):
        if n % cand == 0:
            best = cand
    assert best is not None, (n, target, mult)
    return best


def _cparams(sem):
    return pltpu.CompilerParams(dimension_semantics=sem, vmem_limit_bytes=VMEM_LIMIT)


def _rms_rows(x, g):
    return x * lax.rsqrt(jnp.mean(x * x, axis=-1, keepdims=True) + RMS_EPS) * g


def _pick_rows(sel_ctx, ref):
    return jnp.where(sel_ctx, ref[0:1, :], ref[1:2, :])


ROW_CHUNK = 256


def _for_row_chunks(n_rows, row0, ctx_rows, fn):
    rc = _tile(n_rows, ROW_CHUNK, 16)

    def body(r, carry):
        start = pl.multiple_of(r * rc, rc)
        is_ctx = (row0 + start + lax.broadcasted_iota(jnp.int32, (rc, 1), 0)) < ctx_rows
        fn(pl.ds(start, rc), is_ctx)
        return carry

    lax.fori_loop(0, n_rows // rc, body, 0)


def _normmod_store(h_ref, g_ref, shift_ref, scale_ref, a_scr, *, row0, ctx_rows):
    def fn(rows, is_ctx):
        y = _rms_rows(h_ref[rows, :], g_ref[...])
        a_scr[rows, :] = (y * (1.0 + _pick_rows(is_ctx, scale_ref)) + _pick_rows(is_ctx, shift_ref)).astype(BF16)

    _for_row_chunks(h_ref.shape[0], row0, ctx_rows, fn)


def _gated_resid_store(o_ref, h_ref, g_ref, gate_ref, *, row0, ctx_rows):
    def fn(rows, is_ctx):
        o_ref[rows, :] = h_ref[rows, :] + _pick_rows(is_ctx, gate_ref) * _rms_rows(o_ref[rows, :], g_ref[...])

    _for_row_chunks(o_ref.shape[0], row0, ctx_rows, fn)


def _mod_kernel(c_ref, w_ref, b_ref, o_ref):
    a = c_ref[...]
    a = a * jax.nn.sigmoid(a)
    for n0 in range(0, w_ref.shape[2], LANE):
        cols = slice(n0, n0 + LANE)
        w = w_ref[0, :, cols]
        rows = [jnp.sum(w * a[r], axis=0, keepdims=True) for r in range(2)]
        o_ref[0, :, cols] = jnp.concatenate(rows, axis=0) + b_ref[0, :, cols]


def _modulation(c_ctx, c, w_mod, b_mod):
    depth, d, n = w_mod.shape
    tn = _tile(n, 2048, LANE)
    c_lanes = jnp.broadcast_to(jnp.stack([c_ctx, c])[:, :, None], (2, d, LANE)).astype(F32)
    return pl.pallas_call(
        _mod_kernel,
        grid=(depth, n // tn),
        in_specs=[pl.BlockSpec((2, d, LANE), lambda l, j: (0, 0, 0)),
                  pl.BlockSpec((1, d, tn), lambda l, j: (l, 0, j)),
                  pl.BlockSpec((1, 1, tn), lambda l, j: (l, 0, j))],
        out_specs=pl.BlockSpec((1, 2, tn), lambda l, j: (l, 0, j)),
        out_shape=jax.ShapeDtypeStruct((depth, 2, n), F32),
        compiler_params=_cparams(("arbitrary", "arbitrary")),
        name="modulation",
    )(c_lanes, w_mod, b_mod.reshape(depth, 1, n))


def _normmod_matmul_kernel(h_ref, g_ref, shift_ref, scale_ref, w_ref, o_ref, a_scr, *, ctx_rows, row_step):
    @pl.when(pl.program_id(1) == 0)
    def _():
        _normmod_store(h_ref, g_ref, shift_ref, scale_ref, a_scr,
                       row0=pl.program_id(0) * row_step, ctx_rows=ctx_rows)

    o_ref[...] = jnp.dot(a_scr[...], w_ref[...].astype(BF16), preferred_element_type=F32).astype(o_ref.dtype)


ACC_COLS = 512


def _accumulate(o_ref, lhs, w_ref, k):
    @pl.when(k == 0)
    def _():
        o_ref[...] = jnp.zeros(o_ref.shape, F32)

    for n0 in range(0, o_ref.shape[1], ACC_COLS):
        cols = slice(n0, n0 + ACC_COLS)
        o_ref[:, cols] += jnp.dot(lhs, w_ref[:, cols].astype(BF16), preferred_element_type=F32)


def _normmod_matmul(h, g, shift, scale, w, layer, *, n_ctx, tm, tn, name, cols=None):
    m, d = h.shape
    col0, n = (0, w.shape[2]) if cols is None else cols
    assert col0 % tn == 0 and n % tn == 0
    cb0 = col0 // tn
    return pl.pallas_call(
        functools.partial(_normmod_matmul_kernel, ctx_rows=n_ctx, row_step=tm),
        grid=(m // tm, n // tn),
        in_specs=[pl.BlockSpec((tm, d), lambda i, j: (i, 0), pipeline_mode=pl.Buffered(1)),
                  pl.BlockSpec((1, d), lambda i, j: (0, 0)),
                  pl.BlockSpec((2, d), lambda i, j: (0, 0)),
                  pl.BlockSpec((2, d), lambda i, j: (0, 0)),
                  pl.BlockSpec((None, d, tn), lambda i, j: (layer, 0, cb0 + j))],
        out_specs=pl.BlockSpec((tm, tn), lambda i, j: (i, j)),
        out_shape=jax.ShapeDtypeStruct((m, n), F32),
        scratch_shapes=[pltpu.VMEM((tm, d), BF16)],
        compiler_params=_cparams(("arbitrary", "arbitrary")),
        name=name,
    )(h, g.reshape(1, d), shift, scale, w)


def _matmul_resid_kernel(a_ref, w_ref, h_ref, g_ref, gate_ref, o_ref, *, n_ctx, tm, nk):
    k = pl.program_id(1)
    _accumulate(o_ref, a_ref[...], w_ref, k)

    @pl.when(k == nk - 1)
    def _():
        _gated_resid_store(o_ref, h_ref, g_ref, gate_ref, row0=pl.program_id(0) * tm, ctx_rows=n_ctx)


def _matmul_resid(a, w, layer, h, g, gate, *, n_ctx, tm, tk, name):
    m, kdim = a.shape
    d = w.shape[2]
    nk = kdim // tk
    return pl.pallas_call(
        functools.partial(_matmul_resid_kernel, n_ctx=n_ctx, tm=tm, nk=nk),
        grid=(m // tm, nk),
        in_specs=[pl.BlockSpec((tm, tk), lambda i, k: (i, k)),
                  pl.BlockSpec((None, tk, d), lambda i, k: (layer, k, 0)),
                  pl.BlockSpec((tm, d), lambda i, k: (i, 0)),
                  pl.BlockSpec((1, d), lambda i, k: (0, 0)),
                  pl.BlockSpec((2, d), lambda i, k: (0, 0))],
        out_specs=pl.BlockSpec((tm, d), lambda i, k: (i, 0)),
        out_shape=jax.ShapeDtypeStruct((m, d), F32),
        compiler_params=_cparams(("arbitrary", "arbitrary")),
        name=name,
    )(a, w, h, g.reshape(1, d), gate)


def _mlp_kernel(h_ref, g_in_ref, shift_ref, scale_ref, wup_ref, wdn_ref, g_out_ref, gate_ref, o_ref, a_scr,
                *, n_ctx, tm, nj):
    row0 = pl.program_id(0) * tm
    j = pl.program_id(1)

    @pl.when(j == 0)
    def _():
        _normmod_store(h_ref, g_in_ref, shift_ref, scale_ref, a_scr, row0=row0, ctx_rows=n_ctx)

    hid = jnp.dot(a_scr[...], wup_ref[...].astype(BF16), preferred_element_type=F32)
    hid = jnp.square(jnp.maximum(hid, 0.0)).astype(BF16)
    _accumulate(o_ref, hid, wdn_ref, j)

    @pl.when(j == nj - 1)
    def _():
        _gated_resid_store(o_ref, h_ref, g_out_ref, gate_ref, row0=row0, ctx_rows=n_ctx)


def _mlp_resid(h, g_in, shift, scale, w_up, w_down, layer, g_out, gate, *, n_ctx, tm, tn):
    m, d = h.shape
    hidden = w_up.shape[2]
    nj = hidden // tn
    row = lambda i, j: (i, 0)
    fixed = lambda i, j: (0, 0)
    return pl.pallas_call(
        functools.partial(_mlp_kernel, n_ctx=n_ctx, tm=tm, nj=nj),
        grid=(m // tm, nj),
        in_specs=[pl.BlockSpec((tm, d), row, pipeline_mode=pl.Buffered(1)),
                  pl.BlockSpec((1, d), fixed),
                  pl.BlockSpec((2, d), fixed),
                  pl.BlockSpec((2, d), fixed),
                  pl.BlockSpec((None, d, tn), lambda i, j: (layer, 0, j)),
                  pl.BlockSpec((None, tn, d), lambda i, j: (layer, j, 0)),
                  pl.BlockSpec((1, d), fixed),
                  pl.BlockSpec((2, d), fixed)],
        out_specs=pl.BlockSpec((tm, d), row, pipeline_mode=pl.Buffered(1)),
        out_shape=jax.ShapeDtypeStruct((m, d), F32),
        scratch_shapes=[pltpu.VMEM((tm, d), BF16)],
        compiler_params=_cparams(("arbitrary", "arbitrary")),
        name="mlp",
    )(h, g_in.reshape(1, d), shift, scale, w_up, w_down, g_out.reshape(1, d), gate)


def _rope_tables(n_ctx, seq_len):
    rows = seq_len // GRID_W
    row = jnp.repeat(jnp.arange(rows, dtype=F32), GRID_W)
    col = jnp.tile(jnp.arange(GRID_W, dtype=F32), rows)
    inv = ROPE_BASE ** (-jnp.arange(0, ROPE_AXIS_DIM, 2, dtype=F32) / ROPE_AXIS_DIM)
    ang_r = row[:, None] * inv
    ang_c = col[:, None] * inv
    ang = jnp.concatenate([ang_r, ang_r, ang_c, ang_c], axis=-1)
    cos = jnp.concatenate([jnp.ones((n_ctx, DIFF_QK_DIM), F32), jnp.cos(ang)], axis=0)
    sin = jnp.concatenate([jnp.zeros((n_ctx, DIFF_QK_DIM), F32), jnp.sin(ang)], axis=0)
    sign = jnp.where((jnp.arange(DIFF_QK_DIM) % 32) < 16, -1.0, 1.0).astype(F32)
    sin = sin * sign
    return jnp.tile(cos, (1, 2)), jnp.tile(sin, (1, 2))


def _rope_kernel(p_ref, cos_ref, sin_ref, q_ref, k_ref, vt_ref):
    cos = cos_ref[...]
    sin = sin_ref[...]
    low = (lax.broadcasted_iota(jnp.int32, (1, LANE), 1) % 32) < 16

    def rope(x):
        rot = jnp.where(low, pltpu.roll(x, LANE - 16, axis=1), pltpu.roll(x, 16, axis=1))
        return x * cos + rot * sin

    for c in range(DIFF_QK_W // LANE):
        sl = slice(c * LANE, (c + 1) * LANE)
        q_ref[:, sl] = (rope(p_ref[:, sl]) * (DIFF_QK_DIM ** -0.5 * LOG2_E)).astype(BF16)
        k_ref[:, sl] = rope(p_ref[:, DIFF_QK_W + c * LANE:DIFF_QK_W + (c + 1) * LANE]).astype(BF16)
    for h in range(DIFF_HEADS):
        v = p_ref[:, 2 * DIFF_QK_W + h * DIFF_V_DIM:2 * DIFF_QK_W + (h + 1) * DIFF_V_DIM]
        vt_ref[h, 0:DIFF_V_DIM, :] = v.T.astype(BF16)
        vt_ref[h, DIFF_V_DIM:VT_ROWS, :] = jnp.ones((VT_ROWS - DIFF_V_DIM, v.shape[0]), BF16)


def _rope_split(p, cos, sin, *, tm):
    m = p.shape[0]
    w = 2 * DIFF_QK_W + DIFF_V_W
    out = jax.ShapeDtypeStruct((m, DIFF_QK_W), BF16)
    return pl.pallas_call(
        _rope_kernel,
        grid=(m // tm,),
        in_specs=[pl.BlockSpec((tm, w), lambda i: (i, 0)),
                  pl.BlockSpec((tm, LANE), lambda i: (i, 0)),
                  pl.BlockSpec((tm, LANE), lambda i: (i, 0))],
        out_specs=[pl.BlockSpec((tm, DIFF_QK_W), lambda i: (i, 0)),
                   pl.BlockSpec((tm, DIFF_QK_W), lambda i: (i, 0)),
                   pl.BlockSpec((DIFF_HEADS, VT_ROWS, tm), lambda i: (0, 0, i))],
        out_shape=[out, out, jax.ShapeDtypeStruct((DIFF_HEADS, VT_ROWS, m), BF16)],
        compiler_params=_cparams(("arbitrary",)),
        name="rope_split",
    )(p, cos, sin)


def _attn_kernel(lam_ref, q0_ref, q1_ref, k0_ref, k1_ref, vt_ref, subln_ref, o_ref, acc_scr, s_scr,
                 *, tk, nk, out_scale):
    tq = q0_ref.shape[0]
    lam = lam_ref[0, 0]
    lane_head = lax.broadcasted_iota(jnp.int32, (1, LANE), 1) // DIFF_QK_DIM
    nt = (((1,), (1,)), ((), ()))
    k_refs = (k0_ref, k1_ref)

    def keys(kb):
        start = kb * tk
        return pl.ds(start if isinstance(kb, int) else pl.multiple_of(start, tk), tk)

    for sub in range(2):
        qs = [jnp.where(lane_head == sub, q_ref[...], jnp.zeros((), BF16)) for q_ref in (q0_ref, q1_ref)]
        acc_scr[...] = jnp.zeros(acc_scr.shape, F32)

        def scores(kb, slot):
            for mp in range(2):
                s_scr[slot, mp] = lax.dot_general(k_refs[mp][keys(kb), :], qs[mp], nt,
                                                  preferred_element_type=F32)

        def softmax_pv(kb, slot, ms):
            vt = vt_ref[sub, :, keys(kb)]
            new_ms = []
            for mp in range(2):
                s = s_scr[slot, mp]
                m_new = jnp.maximum(ms[mp], jnp.max(s, axis=0, keepdims=True))
                alpha = jnp.exp2(ms[mp] - m_new)
                p = jnp.exp2((s - m_new).astype(BF16))
                acc_scr[mp] = alpha * acc_scr[mp] + jnp.dot(vt, p, preferred_element_type=F32)
                new_ms.append(m_new)
            return tuple(new_ms)

        def pair(i, ms):
            scores(2 * i + 1, 1)
            ms = softmax_pv(2 * i, 0, ms)
            scores(2 * i + 2, 0)
            return softmax_pv(2 * i + 1, 1, ms)

        m_init = jnp.full((1, tq), -jnp.inf, F32)
        scores(0, 0)
        ms = lax.fori_loop(0, (nk - 1) // 2, pair, (m_init, m_init))
        if (nk - 1) % 2 == 1:
            scores(nk - 1, 1)
            ms = softmax_pv(nk - 2, 0, ms)
            softmax_pv(nk - 1, 1, ms)
        else:
            softmax_pv(nk - 1, 0, ms)
        o0 = acc_scr[0, 0:DIFF_V_DIM, :] / acc_scr[0, DIFF_V_DIM:DIFF_V_DIM + 1, :]
        o1 = acc_scr[1, 0:DIFF_V_DIM, :] / acc_scr[1, DIFF_V_DIM:DIFF_V_DIM + 1, :]
        o = o0 - lam * o1
        o = o * lax.rsqrt(jnp.mean(o * o, axis=0, keepdims=True) + RMS_EPS) * (subln_ref[...] * out_scale)
        o_ref[:, sub * DIFF_V_DIM:(sub + 1) * DIFF_V_DIM] = o.T.astype(o_ref.dtype)


def _diff_attention(lam, q, k, vt, subln, *, q_row0, n_q, n_k, tq, tk, lam_init, name):
    pairs = DIFF_HEADS // 2
    assert q_row0 % tq == 0 and n_q % tq == 0
    qb0 = q_row0 // tq
    nk = n_k // tk
    return pl.pallas_call(
        functools.partial(_attn_kernel, tk=tk, nk=nk, out_scale=1.0 - lam_init),
        grid=(pairs, n_q // tq),
        in_specs=[pl.BlockSpec(memory_space=pltpu.SMEM),
                  pl.BlockSpec((tq, LANE), lambda j, i: (qb0 + i, j)),
                  pl.BlockSpec((tq, LANE), lambda j, i: (qb0 + i, pairs + j)),
                  pl.BlockSpec((n_k, LANE), lambda j, i: (0, j)),
                  pl.BlockSpec((n_k, LANE), lambda j, i: (0, pairs + j)),
                  pl.BlockSpec((2, VT_ROWS, n_k), lambda j, i: (j, 0, 0)),
                  pl.BlockSpec((DIFF_V_DIM, 1), lambda j, i: (0, 0))],
        out_specs=pl.BlockSpec((tq, 2 * DIFF_V_DIM), lambda j, i: (i, j)),
        out_shape=jax.ShapeDtypeStruct((n_q, DIFF_V_W), BF16),
        scratch_shapes=[pltpu.VMEM((2, VT_ROWS, tq), F32), pltpu.VMEM((2, 2, tk, tq), F32)],
        compiler_params=_cparams(("arbitrary", "arbitrary")),
        name=name,
    )(lam, q, q, k, k, vt, subln.reshape(DIFF_V_DIM, 1).astype(F32))


def _s5_operators(lam_re, lam_im, log_dt, b_re, b_im, c_re, c_im, d_skip):
    t = S5_T
    steps = jnp.arange(t + 1, dtype=F32)
    ops = {}
    m_all = None
    for d in range(2):
        lam = lax.complex(lam_re[d].astype(F32), lam_im[d].astype(F32))
        dt = jnp.exp(log_dt[d].astype(F32))[:, None]
        lam_dt = lam * dt
        lam_bar = jnp.exp(lam_dt)
        b_bar = ((lam_bar - 1) / lam)[..., None] * lax.complex(b_re[d].astype(F32), b_im[d].astype(F32))
        c_mat = lax.complex(c_re[d].astype(F32), c_im[d].astype(F32))
        pw = jnp.exp(lam_dt[None] * steps[:, None, None])
        kern = jnp.real(jnp.einsum('gpn,tgn,gnq->tgpq', c_mat, pw[:t], b_bar,
                                   precision=lax.Precision.HIGHEST))
        s_idx = jnp.arange(t)[:, None]
        t_idx = jnp.arange(t)[None, :]
        lag = (t_idx - s_idx) if d == 0 else (s_idx - t_idx)
        onehot = (lag[None] == jnp.arange(t)[:, None, None]).astype(F32)
        m_d = jnp.einsum('xst,xgpq->gsqtp', onehot, kern,
                         precision=lax.Precision.HIGHEST).reshape(S5_GROUPS, S5_ROW, S5_ROW)
        m_all = m_d if m_all is None else m_all + m_d
        pw_in = pw[:t][::-1] if d == 0 else pw[:t]
        vin = pw_in[:, :, :, None] * b_bar[None]
        vin = jnp.transpose(vin, (1, 0, 3, 2)).reshape(S5_GROUPS, S5_ROW, S5_STATE)
        v_d = jnp.concatenate([jnp.real(vin), jnp.imag(vin)], axis=-1)
        pw_out = pw[1:] if d == 0 else pw[1:][::-1]
        wout = c_mat[None] * pw_out[:, :, None, :]
        wout = jnp.transpose(wout, (1, 3, 0, 2)).reshape(S5_GROUPS, S5_STATE, S5_ROW)
        w_d = jnp.concatenate([jnp.real(wout), -jnp.imag(wout)], axis=1)
        kk = (2.0 ** jnp.arange(10, dtype=F32)) * t
        ak = jnp.exp(lam_dt[:, None, :] * kk[None, :, None])
        a1 = jnp.concatenate([jnp.real(ak), jnp.real(ak)], axis=-1)
        a2 = jnp.concatenate([-jnp.imag(ak), jnp.imag(ak)], axis=-1)
        ops[d] = (v_d, w_d, a1, a2)
    eye = jnp.eye(S5_ROW, dtype=F32)
    dvec = jnp.tile(d_skip.astype(F32), (1, t))
    m_all = m_all + eye[None] * dvec[:, None, :]
    v_all = jnp.concatenate([ops[0][0], ops[1][0]], axis=-1)
    w_all = jnp.concatenate([ops[0][1], ops[1][1]], axis=1)
    pw_all = jnp.stack([ops[0][2], ops[0][3], ops[1][2], ops[1][3]], axis=1)
    return m_all.astype(BF16), v_all.astype(BF16), w_all.astype(BF16), pw_all


def _shift_rows(x, s, up):
    n = x.shape[0]
    if s >= n:
        return jnp.zeros_like(x)
    if s % 8 == 0:
        z = jnp.zeros((s, x.shape[1]), x.dtype)
        return jnp.concatenate([x[s:], z], axis=0) if up else jnp.concatenate([z, x[:n - s]], axis=0)
    row = lax.broadcasted_iota(jnp.int32, x.shape, 0)
    if up:
        return jnp.where(row < n - s, pltpu.roll(x, n - s, axis=0), 0.0)
    return jnp.where(row >= s, pltpu.roll(x, s, axis=0), 0.0)


def _s5_chunk_scan(z, a1_ref, a2_ref, up):
    x = _shift_rows(z, 1, up)
    n = x.shape[0]
    k = 0
    while (1 << k) < n:
        xs = _shift_rows(x, 1 << k, up)
        x = x + a1_ref[k:k + 1, :] * xs + a2_ref[k:k + 1, :] * pltpu.roll(xs, S5_STATE, axis=1)
        k += 1
    return x


def _s5_regroup_matrix():
    n = S5_T * LANE
    src = jnp.arange(n)
    t, g, q = src // LANE, (src % LANE) // S5_P, src % S5_P
    dst = g * S5_ROW + t * S5_P + q
    return (dst[:, None] == jnp.arange(n)[None, :]).astype(BF16)


def _s5_kernel(x_ref, perm_ref, m_ref, v_ref, w_ref, pw_ref, y_ref, *, n_ctx_chunks):
    nc = n_ctx_chunks
    rows = x_ref.shape[0] // S5_T
    perm = perm_ref[...]
    x = jnp.concatenate([x_ref[pl.ds(t, rows, stride=S5_T), :].astype(BF16) for t in range(S5_T)],
                        axis=1)
    xp = jnp.dot(x, perm, preferred_element_type=F32).astype(BF16)
    ys = []
    for g in range(S5_GB):
        u = xp[:, g * S5_ROW:(g + 1) * S5_ROW]
        y = jnp.dot(u, m_ref[g], preferred_element_type=F32)
        z = jnp.dot(u, v_ref[g], preferred_element_type=F32)
        zf = z[:, :LANE]
        zb = z[:, LANE:]
        sf = _s5_chunk_scan(zf, pw_ref.at[g, 0], pw_ref.at[g, 1], up=False)
        zb = jnp.concatenate([zb[nc:], zb[:nc]], axis=0)
        sb = _s5_chunk_scan(zb, pw_ref.at[g, 2], pw_ref.at[g, 3], up=True)
        nl = sb.shape[0] - nc
        sb = jnp.concatenate([sb[nl:], sb[:nl]], axis=0)
        s = jnp.concatenate([sf, sb], axis=1).astype(BF16)
        ys.append(y + jnp.dot(s, w_ref[g], preferred_element_type=F32))
    y = jnp.concatenate(ys, axis=1)
    yt = lax.dot_general(y.astype(BF16), perm, (((1,), (1,)), ((), ())),
                         preferred_element_type=F32)
    for t in range(S5_T):
        y_ref[pl.ds(t, rows, stride=S5_T), :] = yt[:, t * LANE:(t + 1) * LANE]


def _s5_scan(p, col0, m_all, v_all, w_all, pw_all, *, n_ctx_chunks):
    m = p.shape[0]
    cb0 = col0 // LANE
    spec = pl.BlockSpec((S5_GB, S5_ROW, S5_ROW), lambda i: (i, 0, 0))
    n = S5_T * LANE
    return pl.pallas_call(
        functools.partial(_s5_kernel, n_ctx_chunks=n_ctx_chunks),
        grid=(S5_WIDTH // LANE,),
        in_specs=[pl.BlockSpec((m, LANE), lambda i: (0, cb0 + i)),
                  pl.BlockSpec((n, n), lambda i: (0, 0)), spec, spec, spec,
                  pl.BlockSpec((S5_GB, 4, 10, LANE), lambda i: (i, 0, 0, 0))],
        out_specs=pl.BlockSpec((m, LANE), lambda i: (0, i)),
        out_shape=jax.ShapeDtypeStruct((m, S5_WIDTH), F32),
        compiler_params=_cparams(("arbitrary",)),
        name="s5_scan",
    )(p, _s5_regroup_matrix(), m_all, v_all, w_all, pw_all)


def _glu_kernel(y_ref, w_ref, b_ref, o_ref):
    g = jax.nn.gelu(y_ref[...])
    gate = jnp.dot(g.astype(BF16), w_ref[...], preferred_element_type=F32) + b_ref[...]
    o_ref[...] = (g * jax.nn.sigmoid(gate)).astype(o_ref.dtype)


def _s5_glu(y, w_glu, b_glu, *, tm):
    m, w = y.shape
    return pl.pallas_call(
        _glu_kernel,
        grid=(m // tm,),
        in_specs=[pl.BlockSpec((tm, w), lambda i: (i, 0)),
                  pl.BlockSpec((w, w), lambda i: (0, 0)),
                  pl.BlockSpec((1, w), lambda i: (0, 0))],
        out_specs=pl.BlockSpec((tm, w), lambda i: (i, 0)),
        out_shape=jax.ShapeDtypeStruct((m, w), BF16),
        compiler_params=_cparams(("arbitrary",)),
        name="s5_glu",
    )(y, w_glu.astype(BF16), b_glu.reshape(1, w).astype(F32))


CONV_HALO = 8


def _conv_kernel(x_ref, w_ref, b_ref, o_ref, pad_scr, *, n_ctx, tile):
    m = x_ref.shape[0]
    half = SSD_CONV // 2
    zeros = jnp.zeros((CONV_HALO, LANE), F32)
    segs = ((0, n_ctx, CONV_HALO), (n_ctx, m, 2 * CONV_HALO))
    pad_scr[0:CONV_HALO, :] = zeros
    pad_scr[CONV_HALO + n_ctx:2 * CONV_HALO + n_ctx, :] = zeros
    pad_scr[2 * CONV_HALO + m:3 * CONV_HALO + m, :] = zeros
    for lo, hi, off in segs:
        pad_scr[lo + off:hi + off, :] = x_ref[lo:hi, :]
    bias = b_ref[...]
    for lo, hi, off in segs:
        for r0 in range(lo, hi, tile):
            acc = bias
            for tap in range(SSD_CONV):
                acc = acc + w_ref[tap:tap + 1, :] * pad_scr[r0 + off + tap - half:r0 + off + tap - half + tile, :]
            o_ref[r0:r0 + tile, :] = acc * jax.nn.sigmoid(acc)


def _conv_silu(p, conv_w, conv_b, *, n_ctx, col0):
    m = p.shape[0]
    cb0 = col0 // LANE
    return pl.pallas_call(
        functools.partial(_conv_kernel, n_ctx=n_ctx, tile=256),
        grid=(SSD_CONV_CH // LANE,),
        in_specs=[pl.BlockSpec((m, LANE), lambda j: (0, cb0 + j)),
                  pl.BlockSpec((SSD_CONV, LANE), lambda j: (0, j)),
                  pl.BlockSpec((1, LANE), lambda j: (0, j))],
        out_specs=pl.BlockSpec((m, LANE), lambda j: (0, j)),
        out_shape=jax.ShapeDtypeStruct((m, SSD_CONV_CH), F32),
        scratch_shapes=[pltpu.VMEM((m + 3 * CONV_HALO, LANE), F32)],
        compiler_params=_cparams(("arbitrary",)),
        name="conv_silu",
    )(p, conv_w, conv_b.reshape(1, SSD_CONV_CH))


def _split_dot(x, e2):
    hi = x.astype(BF16)
    lo = (x - hi.astype(F32)).astype(BF16)
    return jnp.dot(jnp.concatenate([hi, lo], axis=1), e2, preferred_element_type=F32)


def _split3(x):
    x1 = x.astype(BF16)
    r1 = x - x1.astype(F32)
    x2 = r1.astype(BF16)
    x3 = (r1 - x2.astype(F32)).astype(BF16)
    return x1, x2, x3


def _ssd_kernel(x_ref, b_ref, c_ref, dt_ref, bias_ref, a_ref, tri_ref, e_ref, y_ref, state_scr):
    t = x_ref.shape[0]
    backward = pl.program_id(0) == 1

    @pl.when(pl.program_id(1) == 0)
    def _():
        state_scr[...] = jnp.zeros(state_scr.shape, F32)

    tri = tri_ref[0]
    expand = e_ref[...]
    dt = jax.nn.softplus(dt_ref[...] + bias_ref[...])
    a = dt * a_ref[...]
    dt = jnp.where(backward, pltpu.roll(dt, SSD_HEADS, axis=1), dt)
    a = jnp.where(backward, pltpu.roll(a, SSD_HEADS, axis=1), a)
    at = a.T
    cs = sum(jnp.dot(tri, ai, preferred_element_type=F32) for ai in _split3(a))
    cst = sum(lax.dot_general(ai, tri, (((1,), (1,)), ((), ())), preferred_element_type=F32)
              for ai in _split3(at))
    total = jnp.sum(a, axis=0, keepdims=True)
    dt_x = _split_dot(dt, expand)
    din_x = _split_dot(jnp.exp(total - cs), expand)
    dout_x = _split_dot(jnp.exp(cs), expand)
    cdec_x = _split_dot(jnp.broadcast_to(jnp.exp(total), (8, LANE)), expand)[0:1]

    xdt = x_ref[...] * dt_x
    xin = (xdt * din_x).astype(BF16)
    xdt = xdt.astype(BF16)
    lane_head = lax.broadcasted_iota(jnp.int32, (1, LANE), 1) // SSD_HEAD_DIM
    gw = SSD_HPG * SSD_HEAD_DIM
    for g in range(SSD_GROUPS):
        bf = b_ref[:, g * SSD_STATE:(g + 1) * SSD_STATE]
        bg = bf.astype(BF16)
        bgt = bf.T.astype(BF16)
        cg = c_ref[:, g * SSD_STATE:(g + 1) * SSD_STATE].astype(BF16)
        cb = lax.dot_general(cg, bg, (((1,), (1,)), ((), ())), preferred_element_type=F32)
        gs = slice(g * gw, (g + 1) * gw)
        st = state_scr[:, gs]
        y_off = jnp.dot(cg, st.astype(BF16), preferred_element_type=F32) * dout_x[:, gs]
        state_scr[:, gs] = st * cdec_x[:, gs] + jnp.dot(bgt, xin[:, gs], preferred_element_type=F32)
        for pr in range(SSD_HPG // 2):
            cols = slice(g * gw + pr * LANE, g * gw + (pr + 1) * LANE)
            xp = xdt[:, cols]
            mms, xhs = [], []
            for sub in range(2):
                h = g * SSD_HPG + pr * 2 + sub
                seg = jnp.exp(cs[:, h:h + 1] - cst[h:h + 1, :])
                mms.append(jnp.where(tri > 0, cb * seg, 0.0).astype(BF16))
                xhs.append(jnp.where(lane_head == sub, xp, jnp.zeros((), BF16)))
            yd = jnp.dot(jnp.concatenate(mms, axis=1), jnp.concatenate(xhs, axis=0),
                         preferred_element_type=F32)
            y_ref[0, :, cols] = yd + y_off[:, pr * LANE:(pr + 1) * LANE]


def _ssd_scan(xbc, dt_raw, bias, a_neg, *, n_ctx):
    m = xbc.shape[0]
    t = SSD_CHUNK
    nchunks = m // t
    ctx_chunks = n_ctx // t

    def pos(d, c):
        back = jnp.where(c < ctx_chunks, ctx_chunks - 1 - c, nchunks - 1 + ctx_chunks - c)
        return jnp.where(d == 0, c, back)

    idx = jnp.arange(t)
    tri = jnp.stack([idx[:, None] >= idx[None, :], idx[:, None] <= idx[None, :]]).astype(BF16)
    expand = (jnp.arange(LANE)[:, None] == (jnp.arange(SSD_INNER)[None, :] // SSD_HEAD_DIM)).astype(BF16)
    expand = jnp.concatenate([expand, expand], axis=0)
    xb = SSD_INNER // 1024
    return pl.pallas_call(
        _ssd_kernel,
        grid=(2, nchunks),
        in_specs=[pl.BlockSpec((t, SSD_INNER), lambda d, c: (pos(d, c), 0)),
                  pl.BlockSpec((t, 1024), lambda d, c: (pos(d, c), xb)),
                  pl.BlockSpec((t, 1024), lambda d, c: (pos(d, c), xb + 1)),
                  pl.BlockSpec((t, LANE), lambda d, c: (pos(d, c), 0)),
                  pl.BlockSpec((1, LANE), lambda d, c: (0, 0)),
                  pl.BlockSpec((1, LANE), lambda d, c: (0, 0)),
                  pl.BlockSpec((1, t, t), lambda d, c: (d, 0, 0)),
                  pl.BlockSpec((2 * LANE, SSD_INNER), lambda d, c: (0, 0))],
        out_specs=pl.BlockSpec((1, t, SSD_INNER), lambda d, c: (d, pos(d, c), 0)),
        out_shape=jax.ShapeDtypeStruct((2, m, SSD_INNER), F32),
        scratch_shapes=[pltpu.VMEM((SSD_STATE, SSD_INNER), F32)],
        compiler_params=_cparams(("arbitrary", "arbitrary")),
        name="ssd_scan",
    )(xbc, xbc, xbc, dt_raw, bias.reshape(1, LANE), a_neg.reshape(1, LANE), tri, expand)


def _ssd_finish_kernel(y_ref, x_ref, z_ref, d_ref, nw_ref, o_ref):
    z = z_ref[...]
    y = (x_ref[...] * d_ref[...] + y_ref[0] + y_ref[1]) * (z * jax.nn.sigmoid(z))
    o_ref[...] = _rms_rows(y, nw_ref[...]).astype(o_ref.dtype)


def _ssd_finish(y2, xbc, p, d_x, norm_w, *, tm):
    m = xbc.shape[0]
    w = SSD_INNER
    return pl.pallas_call(
        _ssd_finish_kernel,
        grid=(m // tm,),
        in_specs=[pl.BlockSpec((2, tm, w), lambda i: (0, i, 0)),
                  pl.BlockSpec((tm, w), lambda i: (i, 0)),
                  pl.BlockSpec((tm, w), lambda i: (i, 0)),
                  pl.BlockSpec((1, w), lambda i: (0, 0)),
                  pl.BlockSpec((1, w), lambda i: (0, 0))],
        out_specs=pl.BlockSpec((tm, w), lambda i: (i, 0)),
        out_shape=jax.ShapeDtypeStruct((m, w), BF16),
        compiler_params=_cparams(("arbitrary",)),
        name="ssd_finish",
    )(y2, xbc, p, d_x, norm_w.reshape(1, w))


def _even_mixer(hcat, g0, shift, scale, w_in, j, lam_p, subln, s5p, cos, sin, lam_init, *, n_ctx, tm):
    m = hcat.shape[0]
    p = _normmod_matmul(hcat, g0, shift, scale, w_in, j, n_ctx=n_ctx, tm=tm, tn=512,
                        name="in_proj_even")
    q, k, vt = _rope_split(p, cos, sin, tm=_tile(m, 256, LANE))
    lp = lam_p.astype(F32)
    lam = (jnp.exp(jnp.sum(lp[0] * lp[1])) - jnp.exp(jnp.sum(lp[2] * lp[3])) + lam_init).reshape(1, 1)
    tq = _tile(math.gcd(n_ctx, m - n_ctx), 256, LANE)
    o_lat = _diff_attention(lam, q, k, vt, subln, q_row0=n_ctx, n_q=m - n_ctx, n_k=m, tq=tq,
                            tk=_tile(m, 768, LANE), lam_init=lam_init, name="diff_attn")
    o_ctx = _diff_attention(lam, q, k, vt, subln, q_row0=0, n_q=n_ctx, n_k=n_ctx, tq=tq,
                            tk=_tile(n_ctx, 768, LANE), lam_init=lam_init, name="diff_attn_ctx")
    y = _s5_scan(p, 2 * DIFF_QK_W + DIFF_V_W, *_s5_operators(*s5p[:8]), n_ctx_chunks=n_ctx // S5_T)
    s = _s5_glu(y, s5p[8], s5p[9], tm=_tile(m, 1056, 16))
    return jnp.concatenate([jnp.concatenate([o_ctx, o_lat], axis=0), s], axis=-1)


def _odd_mixer(hcat, g0, shift, scale, w_in, j, conv_w, conv_b, dt_bias, a_log, d_skip, norm_w, *, n_ctx, tm):
    main_w = SSD_INNER + SSD_CONV_CH
    p = _normmod_matmul(hcat, g0, shift, scale, w_in, j, n_ctx=n_ctx, tm=tm, tn=512,
                        name="in_proj_odd", cols=(0, main_w))
    dt_raw = _normmod_matmul(hcat, g0, shift, scale, w_in, j, n_ctx=n_ctx, tm=tm, tn=LANE,
                             name="in_proj_dt", cols=(main_w, 2 * SSD_HEADS))
    xbc = _conv_silu(p, conv_w, conv_b, n_ctx=n_ctx, col0=SSD_INNER)
    a_neg = -jnp.exp(a_log.astype(F32))
    y2 = _ssd_scan(xbc, dt_raw, dt_bias.astype(F32), a_neg, n_ctx=n_ctx)
    d_x = jnp.repeat(d_skip.astype(F32), SSD_HEAD_DIM).reshape(1, SSD_INNER)
    return _ssd_finish(y2, xbc, p, d_x, norm_w, tm=_tile(xbc.shape[0], 176, 16))


def kernel(x, c, ctx, c_ctx, w_mod, b_mod, norm_g, w_in_even, w_out_even, diff_lam, diff_subln, s5_lam_re, s5_lam_im, s5_log_dt, s5_b_re, s5_b_im, s5_c_re, s5_c_im, s5_d, s5_w_glu, s5_b_glu, w_in_odd, conv_w, conv_b, ssd_dt_bias, ssd_a_log, ssd_d, ssd_norm_w, w_out_odd, w_up, w_down):
    bsz, seq, d = x.shape
    assert bsz == 1 and d == D_MODEL
    n_ctx = ctx.shape[1]
    hcat = jnp.concatenate([ctx[0], x[0]], axis=0)
    m = hcat.shape[0]
    tm = _tile(m, 2112, 16)
    tm_acc = _tile(m, 1056, 16)
    mods = _modulation(c_ctx, c[0], w_mod, b_mod).reshape(DEPTH, 2, N_MOD, d)
    cos, sin = _rope_tables(n_ctx, seq)
    for i in range(DEPTH):
        md = mods[i]
        g = norm_g[i]
        j = i // 2
        if i % 2 == 0:
            lam_init = 0.8 - 0.6 * math.exp(-0.3 * i)
            s5p = (s5_lam_re[j], s5_lam_im[j], s5_log_dt[j], s5_b_re[j], s5_b_im[j], s5_c_re[j], s5_c_im[j],
                   s5_d[j], s5_w_glu[j], s5_b_glu[j])
            mix = _even_mixer(hcat, g[0], md[:, 0], md[:, 1], w_in_even, j, diff_lam[j],
                              diff_subln[j], s5p, cos, sin, lam_init, n_ctx=n_ctx, tm=tm)
            w_out = w_out_even
        else:
            mix = _odd_mixer(hcat, g[0], md[:, 0], md[:, 1], w_in_odd, j, conv_w[j], conv_b[j], ssd_dt_bias[j],
                             ssd_a_log[j], ssd_d[j], ssd_norm_w[j], n_ctx=n_ctx, tm=tm)
            w_out = w_out_odd
        hcat = _matmul_resid(mix, w_out, j, hcat, g[1], md[:, 2], n_ctx=n_ctx, tm=tm_acc, tk=512, name="out_proj")
        hcat = _mlp_resid(hcat, g[2], md[:, 3], md[:, 4], w_up, w_down, i, g[3], md[:, 5],
                          n_ctx=n_ctx, tm=tm_acc, tn=512)
    return hcat[n_ctx:][None]
```

```python
import functools
import math

import jax
import jax.numpy as jnp
from jax import lax
from jax.experimental import pallas as pl
from jax.experimental.pallas import tpu as pltpu

F32 = jnp.float32
BF16 = jnp.bfloat16

D_MODEL = 2048
DEPTH = 4
GRID_W = 64
N_MOD = 6
RMS_EPS = 1e-6

DIFF_HEADS = 8
DIFF_QK_DIM = 64
DIFF_V_DIM = 128
DIFF_QK_W = 1024
DIFF_V_W = 1024
ROPE_BASE = 10000.0
ROPE_AXIS_DIM = 32
LOG2_E = math.log2(math.e)
VT_ROWS = DIFF_V_DIM + 16

S5_WIDTH = 1024
S5_P = 16
S5_GROUPS = 64
S5_STATE = 64
S5_T = 16
S5_ROW = S5_T * S5_P
S5_GB = 128 // S5_P

SSD_INNER = 4096
SSD_HEAD_DIM = 64
SSD_HEADS = 64
SSD_GROUPS = 8
SSD_HPG = 8
SSD_STATE = 128
SSD_CONV = 5
SSD_CHUNK = 128
SSD_CONV_CH = 6144
ODD_IN_W = 10368

MLP_HIDDEN = 8192

LANE = 128
VMEM_LIMIT = 56 * 1024 * 1024


def _tile(n, target, mult):
    best = None
    for cand in range(mult, min(n, target) + 1, mult):
        if n % cand == 0:
            best = cand
    assert best is not None, (n, target, mult)
    return best


def _cparams(sem):
    return pltpu.CompilerParams(dimension_semantics=sem, vmem_limit_bytes=VMEM_LIMIT)


def _rms_rows(x, g):
    return x * lax.rsqrt(jnp.mean(x * x, axis=-1, keepdims=True) + RMS_EPS) * g


def _pick_rows(sel_ctx, ref):
    return jnp.where(sel_ctx, ref[0:1, :], ref[1:2, :])


ROW_CHUNK = 64


def _for_row_chunks(n_rows, row0, ctx_rows, fn):
    rc = _tile(n_rows, ROW_CHUNK, 16)

    def body(r, carry):
        start = pl.multiple_of(r * rc, rc)
        is_ctx = (row0 + start + lax.broadcasted_iota(jnp.int32, (rc, 1), 0)) < ctx_rows
        fn(pl.ds(start, rc), is_ctx)
        return carry

    lax.fori_loop(0, n_rows // rc, body, 0)


def _normmod_store(h_ref, g_ref, shift_ref, scale_ref, a_scr, *, row0, ctx_rows):
    def fn(rows, is_ctx):
        y = _rms_rows(h_ref[rows, :], g_ref[...])
        a_scr[rows, :] = (y * (1.0 + _pick_rows(is_ctx, scale_ref)) + _pick_rows(is_ctx, shift_ref)).astype(BF16)

    _for_row_chunks(h_ref.shape[0], row0, ctx_rows, fn)


def _gated_resid_store(o_ref, h_ref, g_ref, gate_ref, *, row0, ctx_rows):
    def fn(rows, is_ctx):
        o_ref[rows, :] = h_ref[rows, :] + _pick_rows(is_ctx, gate_ref) * _rms_rows(o_ref[rows, :], g_ref[...])

    _for_row_chunks(o_ref.shape[0], row0, ctx_rows, fn)


def _mod_kernel(c_ref, w_ref, b_ref, o_ref):
    a = c_ref[...]
    a = a * jax.nn.sigmoid(a)
    for n0 in range(0, w_ref.shape[2], LANE):
        cols = slice(n0, n0 + LANE)
        w = w_ref[0, :, cols]
        rows = [jnp.sum(w * a[r], axis=0, keepdims=True) for r in range(2)]
        o_ref[0, :, cols] = jnp.concatenate(rows, axis=0) + b_ref[0, :, cols]


def _modulation(c_ctx, c, w_mod, b_mod):
    depth, d, n = w_mod.shape
    tn = _tile(n, 2048, LANE)
    c_lanes = jnp.broadcast_to(jnp.stack([c_ctx, c])[:, :, None], (2, d, LANE)).astype(F32)
    return pl.pallas_call(
        _mod_kernel,
        grid=(depth, n // tn),
        in_specs=[pl.BlockSpec((2, d, LANE), lambda l, j: (0, 0, 0)),
                  pl.BlockSpec((1, d, tn), lambda l, j: (l, 0, j)),
                  pl.BlockSpec((1, 1, tn), lambda l, j: (l, 0, j))],
        out_specs=pl.BlockSpec((1, 2, tn), lambda l, j: (l, 0, j)),
        out_shape=jax.ShapeDtypeStruct((depth, 2, n), F32),
        compiler_params=_cparams(("arbitrary", "arbitrary")),
        name="modulation",
    )(c_lanes, w_mod, b_mod.reshape(depth, 1, n))


def _normmod_matmul_kernel(h_ref, g_ref, shift_ref, scale_ref, w_ref, *rest, ctx_rows, row_step, tail):
    if tail:
        wt_ref, o_ref, ot_ref, a_scr = rest
    else:
        o_ref, a_scr = rest

    @pl.when(pl.program_id(1) == 0)
    def _():
        _normmod_store(h_ref, g_ref, shift_ref, scale_ref, a_scr,
                       row0=pl.program_id(0) * row_step, ctx_rows=ctx_rows)
        if tail:
            ot_ref[...] = jnp.dot(a_scr[...], wt_ref[...].astype(BF16), preferred_element_type=F32)

    o_ref[...] = jnp.dot(a_scr[...], w_ref[...].astype(BF16), preferred_element_type=F32)


ACC_COLS = 512


def _accumulate(o_ref, lhs, w_ref, k):
    @pl.when(k == 0)
    def _():
        o_ref[...] = jnp.zeros(o_ref.shape, F32)

    for n0 in range(0, o_ref.shape[1], ACC_COLS):
        cols = slice(n0, n0 + ACC_COLS)
        o_ref[:, cols] += jnp.dot(lhs, w_ref[:, cols].astype(BF16), preferred_element_type=F32)


def _normmod_matmul(h, g, shift, scale, w, layer, *, n_ctx, tm, tn, name, n_main=None):
    m, d = h.shape
    n = w.shape[2]
    n_main = n if n_main is None else n_main
    n_tail = n - n_main
    assert n_main % tn == 0 and (n_tail == 0 or n_main % n_tail == 0)
    in_specs = [pl.BlockSpec((tm, d), lambda i, j: (i, 0), pipeline_mode=pl.Buffered(1)),
                pl.BlockSpec((1, d), lambda i, j: (0, 0)),
                pl.BlockSpec((2, d), lambda i, j: (0, 0)),
                pl.BlockSpec((2, d), lambda i, j: (0, 0)),
                pl.BlockSpec((None, d, tn), lambda i, j: (layer, 0, j))]
    out_specs = [pl.BlockSpec((tm, tn), lambda i, j: (i, j))]
    out_shape = [jax.ShapeDtypeStruct((m, n_main), F32)]
    args = [h, g.reshape(1, d), shift, scale, w]
    if n_tail:
        in_specs.append(pl.BlockSpec((None, d, n_tail), lambda i, j: (layer, 0, n_main // n_tail)))
        out_specs.append(pl.BlockSpec((tm, n_tail), lambda i, j: (i, 0)))
        out_shape.append(jax.ShapeDtypeStruct((m, n_tail), F32))
        args.append(w)
    outs = pl.pallas_call(
        functools.partial(_normmod_matmul_kernel, ctx_rows=n_ctx, row_step=tm, tail=bool(n_tail)),
        grid=(m // tm, n_main // tn),
        in_specs=in_specs,
        out_specs=out_specs,
        out_shape=out_shape,
        scratch_shapes=[pltpu.VMEM((tm, d), BF16)],
        compiler_params=_cparams(("arbitrary", "arbitrary")),
        name=name,
    )(*args)
    return outs if n_tail else outs[0]


def _matmul_resid_kernel(a_ref, w_ref, h_ref, g_ref, gate_ref, o_ref, *, n_ctx, tm, nk):
    k = pl.program_id(1)
    _accumulate(o_ref, a_ref[...], w_ref, k)

    @pl.when(k == nk - 1)
    def _():
        _gated_resid_store(o_ref, h_ref, g_ref, gate_ref, row0=pl.program_id(0) * tm, ctx_rows=n_ctx)


def _matmul_resid(a, w, layer, h, g, gate, *, n_ctx, tm, tk, name):
    m, kdim = a.shape
    d = w.shape[2]
    nk = kdim // tk
    return pl.pallas_call(
        functools.partial(_matmul_resid_kernel, n_ctx=n_ctx, tm=tm, nk=nk),
        grid=(m // tm, nk),
        in_specs=[pl.BlockSpec((tm, tk), lambda i, k: (i, k)),
                  pl.BlockSpec((None, tk, d), lambda i, k: (layer, k, 0)),
                  pl.BlockSpec((tm, d), lambda i, k: (i, 0)),
                  pl.BlockSpec((1, d), lambda i, k: (0, 0)),
                  pl.BlockSpec((2, d), lambda i, k: (0, 0))],
        out_specs=pl.BlockSpec((tm, d), lambda i, k: (i, 0)),
        out_shape=jax.ShapeDtypeStruct((m, d), F32),
        compiler_params=_cparams(("arbitrary", "arbitrary")),
        name=name,
    )(a, w, h, g.reshape(1, d), gate)


def _mlp_kernel(h_ref, g_in_ref, shift_ref, scale_ref, wup_ref, wdn_ref, g_out_ref, gate_ref, o_ref, a_scr,
                *, n_ctx, tm, nj):
    row0 = pl.program_id(0) * tm
    j = pl.program_id(1)

    @pl.when(j == 0)
    def _():
        _normmod_store(h_ref, g_in_ref, shift_ref, scale_ref, a_scr, row0=row0, ctx_rows=n_ctx)

    hid = jnp.dot(a_scr[...], wup_ref[...].astype(BF16), preferred_element_type=F32)
    hid = jnp.square(jnp.maximum(hid, 0.0)).astype(BF16)
    _accumulate(o_ref, hid, wdn_ref, j)

    @pl.when(j == nj - 1)
    def _():
        _gated_resid_store(o_ref, h_ref, g_out_ref, gate_ref, row0=row0, ctx_rows=n_ctx)


def _mlp_resid(h, g_in, shift, scale, w_up, w_down, layer, g_out, gate, *, n_ctx, tm, tn):
    m, d = h.shape
    hidden = w_up.shape[2]
    nj = hidden // tn
    row = lambda i, j: (i, 0)
    fixed = lambda i, j: (0, 0)
    return pl.pallas_call(
        functools.partial(_mlp_kernel, n_ctx=n_ctx, tm=tm, nj=nj),
        grid=(m // tm, nj),
        in_specs=[pl.BlockSpec((tm, d), row, pipeline_mode=pl.Buffered(1)),
                  pl.BlockSpec((1, d), fixed),
                  pl.BlockSpec((2, d), fixed),
                  pl.BlockSpec((2, d), fixed),
                  pl.BlockSpec((None, d, tn), lambda i, j: (layer, 0, j)),
                  pl.BlockSpec((None, tn, d), lambda i, j: (layer, j, 0)),
                  pl.BlockSpec((1, d), fixed),
                  pl.BlockSpec((2, d), fixed)],
        out_specs=pl.BlockSpec((tm, d), row, pipeline_mode=pl.Buffered(1)),
        out_shape=jax.ShapeDtypeStruct((m, d), F32),
        scratch_shapes=[pltpu.VMEM((tm, d), BF16)],
        compiler_params=_cparams(("arbitrary", "arbitrary")),
        name="mlp",
    )(h, g_in.reshape(1, d), shift, scale, w_up, w_down, g_out.reshape(1, d), gate)


def _rope_tables(n_ctx, seq_len):
    rows = seq_len // GRID_W
    row = jnp.repeat(jnp.arange(rows, dtype=F32), GRID_W)
    col = jnp.tile(jnp.arange(GRID_W, dtype=F32), rows)
    inv = ROPE_BASE ** (-jnp.arange(0, ROPE_AXIS_DIM, 2, dtype=F32) / ROPE_AXIS_DIM)
    ang_r = row[:, None] * inv
    ang_c = col[:, None] * inv
    ang = jnp.concatenate([ang_r, ang_r, ang_c, ang_c], axis=-1)
    cos = jnp.concatenate([jnp.ones((n_ctx, DIFF_QK_DIM), F32), jnp.cos(ang)], axis=0)
    sin = jnp.concatenate([jnp.zeros((n_ctx, DIFF_QK_DIM), F32), jnp.sin(ang)], axis=0)
    sign = jnp.where((jnp.arange(DIFF_QK_DIM) % 32) < 16, -1.0, 1.0).astype(F32)
    sin = sin * sign
    return jnp.tile(cos, (1, 2)), jnp.tile(sin, (1, 2))


def _rope_kernel(p_ref, cos_ref, sin_ref, q_ref, k_ref, vt_ref):
    cos = cos_ref[...]
    sin = sin_ref[...]
    low = (lax.broadcasted_iota(jnp.int32, (1, LANE), 1) % 32) < 16

    def rope(x):
        rot = jnp.where(low, pltpu.roll(x, LANE - 16, axis=1), pltpu.roll(x, 16, axis=1))
        return x * cos + rot * sin

    for c in range(DIFF_QK_W // LANE):
        sl = slice(c * LANE, (c + 1) * LANE)
        q_ref[:, sl] = (rope(p_ref[:, sl]) * (DIFF_QK_DIM ** -0.5 * LOG2_E)).astype(BF16)
        k_ref[:, sl] = rope(p_ref[:, DIFF_QK_W + c * LANE:DIFF_QK_W + (c + 1) * LANE]).astype(BF16)
    for h in range(DIFF_HEADS):
        v = p_ref[:, 2 * DIFF_QK_W + h * DIFF_V_DIM:2 * DIFF_QK_W + (h + 1) * DIFF_V_DIM]
        vt_ref[h, 0:DIFF_V_DIM, :] = v.T.astype(BF16)
        vt_ref[h, DIFF_V_DIM:VT_ROWS, :] = jnp.ones((VT_ROWS - DIFF_V_DIM, v.shape[0]), BF16)


def _rope_split(p, cos, sin, *, tm):
    m = p.shape[0]
    w = 2 * DIFF_QK_W + DIFF_V_W
    out = jax.ShapeDtypeStruct((m, DIFF_QK_W), BF16)
    return pl.pallas_call(
        _rope_kernel,
        grid=(m // tm,),
        in_specs=[pl.BlockSpec((tm, w), lambda i: (i, 0)),
                  pl.BlockSpec((tm, LANE), lambda i: (i, 0)),
                  pl.BlockSpec((tm, LANE), lambda i: (i, 0))],
        out_specs=[pl.BlockSpec((tm, DIFF_QK_W), lambda i: (i, 0)),
                   pl.BlockSpec((tm, DIFF_QK_W), lambda i: (i, 0)),
                   pl.BlockSpec((DIFF_HEADS, VT_ROWS, tm), lambda i: (0, 0, i))],
        out_shape=[out, out, jax.ShapeDtypeStruct((DIFF_HEADS, VT_ROWS, m), BF16)],
        compiler_params=_cparams(("arbitrary",)),
        name="rope_split",
    )(p, cos, sin)


def _attn_kernel(lam_ref, q0_ref, q1_ref, k0_ref, k1_ref, vt_ref, subln_ref, o_ref, acc_scr, s_scr,
                 *, tk, nk, out_scale):
    tq = q0_ref.shape[0]
    lam = lam_ref[0, 0]
    lane_head = lax.broadcasted_iota(jnp.int32, (1, LANE), 1) // DIFF_QK_DIM
    nt = (((1,), (1,)), ((), ()))
    k_refs = (k0_ref, k1_ref)

    def keys(kb):
        start = kb * tk
        return pl.ds(start if isinstance(kb, int) else pl.multiple_of(start, tk), tk)

    for sub in range(2):
        qs = [jnp.where(lane_head == sub, q_ref[...], jnp.zeros((), BF16)) for q_ref in (q0_ref, q1_ref)]
        acc_scr[...] = jnp.zeros(acc_scr.shape, F32)

        def scores(kb, slot):
            for mp in range(2):
                s_scr[slot, mp] = lax.dot_general(k_refs[mp][keys(kb), :], qs[mp], nt,
                                                  preferred_element_type=F32)

        def softmax_pv(kb, slot, ms):
            vt = vt_ref[sub, :, keys(kb)]
            new_ms = []
            for mp in range(2):
                s = s_scr[slot, mp]
                m_new = jnp.maximum(ms[mp], jnp.max(s, axis=0, keepdims=True))
                alpha = jnp.exp2(ms[mp] - m_new)
                p = jnp.exp2((s - m_new).astype(BF16))
                acc_scr[mp] = alpha * acc_scr[mp] + jnp.dot(vt, p, preferred_element_type=F32)
                new_ms.append(m_new)
            return tuple(new_ms)

        def pair(i, ms):
            scores(2 * i + 1, 1)
            ms = softmax_pv(2 * i, 0, ms)
            scores(2 * i + 2, 0)
            return softmax_pv(2 * i + 1, 1, ms)

        m_init = jnp.full((1, tq), -jnp.inf, F32)
        scores(0, 0)
        ms = lax.fori_loop(0, (nk - 1) // 2, pair, (m_init, m_init))
        if (nk - 1) % 2 == 1:
            scores(nk - 1, 1)
            ms = softmax_pv(nk - 2, 0, ms)
            softmax_pv(nk - 1, 1, ms)
        else:
            softmax_pv(nk - 1, 0, ms)
        o0 = acc_scr[0, 0:DIFF_V_DIM, :] / acc_scr[0, DIFF_V_DIM:DIFF_V_DIM + 1, :]
        o1 = acc_scr[1, 0:DIFF_V_DIM, :] / acc_scr[1, DIFF_V_DIM:DIFF_V_DIM + 1, :]
        o = o0 - lam * o1
        o = o * lax.rsqrt(jnp.mean(o * o, axis=0, keepdims=True) + RMS_EPS) * (subln_ref[...] * out_scale)
        o_ref[:, sub * DIFF_V_DIM:(sub + 1) * DIFF_V_DIM] = o.T.astype(o_ref.dtype)


def _diff_attention(lam, q, k, vt, subln, *, q_row0, n_q, n_k, tq, tk, lam_init, name):
    pairs = DIFF_HEADS // 2
    assert q_row0 % tq == 0 and n_q % tq == 0
    qb0 = q_row0 // tq
    nk = n_k // tk
    return pl.pallas_call(
        functools.partial(_attn_kernel, tk=tk, nk=nk, out_scale=1.0 - lam_init),
        grid=(pairs, n_q // tq),
        in_specs=[pl.BlockSpec(memory_space=pltpu.SMEM),
                  pl.BlockSpec((tq, LANE), lambda j, i: (qb0 + i, j)),
                  pl.BlockSpec((tq, LANE), lambda j, i: (qb0 + i, pairs + j)),
                  pl.BlockSpec((n_k, LANE), lambda j, i: (0, j)),
                  pl.BlockSpec((n_k, LANE), lambda j, i: (0, pairs + j)),
                  pl.BlockSpec((2, VT_ROWS, n_k), lambda j, i: (j, 0, 0)),
                  pl.BlockSpec((DIFF_V_DIM, 1), lambda j, i: (0, 0))],
        out_specs=pl.BlockSpec((tq, 2 * DIFF_V_DIM), lambda j, i: (i, j)),
        out_shape=jax.ShapeDtypeStruct((n_q, DIFF_V_W), BF16),
        scratch_shapes=[pltpu.VMEM((2, VT_ROWS, tq), F32), pltpu.VMEM((2, 2, tk, tq), F32)],
        compiler_params=_cparams(("arbitrary", "arbitrary")),
        name=name,
    )(lam, q, q, k, k, vt, subln.reshape(DIFF_V_DIM, 1).astype(F32))


def _s5_operators(lam_re, lam_im, log_dt, b_re, b_im, c_re, c_im, d_skip):
    t = S5_T
    steps = jnp.arange(t + 1, dtype=F32)
    ops = {}
    m_all = None
    for d in range(2):
        lam = lax.complex(lam_re[d].astype(F32), lam_im[d].astype(F32))
        dt = jnp.exp(log_dt[d].astype(F32))[:, None]
        lam_dt = lam * dt
        lam_bar = jnp.exp(lam_dt)
        b_bar = ((lam_bar - 1) / lam)[..., None] * lax.complex(b_re[d].astype(F32), b_im[d].astype(F32))
        c_mat = lax.complex(c_re[d].astype(F32), c_im[d].astype(F32))
        pw = jnp.exp(lam_dt[None] * steps[:, None, None])
        kern = jnp.real(jnp.einsum('gpn,tgn,gnq->tgpq', c_mat, pw[:t], b_bar,
                                   precision=lax.Precision.HIGHEST))
        s_idx = jnp.arange(t)[:, None]
        t_idx = jnp.arange(t)[None, :]
        lag = (t_idx - s_idx) if d == 0 else (s_idx - t_idx)
        onehot = (lag[None] == jnp.arange(t)[:, None, None]).astype(F32)
        m_d = jnp.einsum('xst,xgpq->gsqtp', onehot, kern,
                         precision=lax.Precision.HIGHEST).reshape(S5_GROUPS, S5_ROW, S5_ROW)
        m_all = m_d if m_all is None else m_all + m_d
        pw_in = pw[:t][::-1] if d == 0 else pw[:t]
        vin = pw_in[:, :, :, None] * b_bar[None]
        vin = jnp.transpose(vin, (1, 0, 3, 2)).reshape(S5_GROUPS, S5_ROW, S5_STATE)
        v_d = jnp.concatenate([jnp.real(vin), jnp.imag(vin)], axis=-1)
        pw_out = pw[1:] if d == 0 else pw[1:][::-1]
        wout = c_mat[None] * pw_out[:, :, None, :]
        wout = jnp.transpose(wout, (1, 3, 0, 2)).reshape(S5_GROUPS, S5_STATE, S5_ROW)
        w_d = jnp.concatenate([jnp.real(wout), -jnp.imag(wout)], axis=1)
        kk = (2.0 ** jnp.arange(10, dtype=F32)) * t
        ak = jnp.exp(lam_dt[:, None, :] * kk[None, :, None])
        a1 = jnp.concatenate([jnp.real(ak), jnp.real(ak)], axis=-1)
        a2 = jnp.concatenate([-jnp.imag(ak), jnp.imag(ak)], axis=-1)
        ops[d] = (v_d, w_d, a1, a2)
    eye = jnp.eye(S5_ROW, dtype=F32)
    dvec = jnp.tile(d_skip.astype(F32), (1, t))
    m_all = m_all + eye[None] * dvec[:, None, :]
    v_all = jnp.concatenate([ops[0][0], ops[1][0]], axis=-1)
    w_all = jnp.concatenate([ops[0][1], ops[1][1]], axis=1)
    pw_all = jnp.stack([ops[0][2], ops[0][3], ops[1][2], ops[1][3]], axis=1)
    return m_all.astype(BF16), v_all.astype(BF16), w_all.astype(BF16), pw_all


def _shift_rows(x, s, up):
    n = x.shape[0]
    if s >= n:
        return jnp.zeros_like(x)
    if s % 8 == 0:
        z = jnp.zeros((s, x.shape[1]), x.dtype)
        return jnp.concatenate([x[s:], z], axis=0) if up else jnp.concatenate([z, x[:n - s]], axis=0)
    row = lax.broadcasted_iota(jnp.int32, x.shape, 0)
    if up:
        return jnp.where(row < n - s, pltpu.roll(x, n - s, axis=0), 0.0)
    return jnp.where(row >= s, pltpu.roll(x, s, axis=0), 0.0)


def _s5_chunk_scan(z, a1_ref, a2_ref, up):
    x = _shift_rows(z, 1, up)
    n = x.shape[0]
    k = 0
    while (1 << k) < n:
        xs = _shift_rows(x, 1 << k, up)
        x = x + a1_ref[k:k + 1, :] * xs + a2_ref[k:k + 1, :] * pltpu.roll(xs, S5_STATE, axis=1)
        k += 1
    return x


def _s5_regroup_matrix():
    n = S5_T * LANE
    src = jnp.arange(n)
    t, g, q = src // LANE, (src % LANE) // S5_P, src % S5_P
    dst = g * S5_ROW + t * S5_P + q
    return (dst[:, None] == jnp.arange(n)[None, :]).astype(BF16)


def _s5_kernel(x_ref, perm_ref, m_ref, v_ref, w_ref, pw_ref, y_ref, *, n_ctx_chunks):
    nc = n_ctx_chunks
    rows = x_ref.shape[0] // S5_T
    perm = perm_ref[...]
    x = jnp.concatenate([x_ref[pl.ds(t, rows, stride=S5_T), :].astype(BF16) for t in range(S5_T)],
                        axis=1)
    xp = jnp.dot(x, perm, preferred_element_type=F32).astype(BF16)
    ys = []
    for g in range(S5_GB):
        u = xp[:, g * S5_ROW:(g + 1) * S5_ROW]
        y = jnp.dot(u, m_ref[g], preferred_element_type=F32)
        z = jnp.dot(u, v_ref[g], preferred_element_type=F32)
        zf = z[:, :LANE]
        zb = z[:, LANE:]
        sf = _s5_chunk_scan(zf, pw_ref.at[g, 0], pw_ref.at[g, 1], up=False)
        zb = jnp.concatenate([zb[nc:], zb[:nc]], axis=0)
        sb = _s5_chunk_scan(zb, pw_ref.at[g, 2], pw_ref.at[g, 3], up=True)
        nl = sb.shape[0] - nc
        sb = jnp.concatenate([sb[nl:], sb[:nl]], axis=0)
        s = jnp.concatenate([sf, sb], axis=1).astype(BF16)
        ys.append(y + jnp.dot(s, w_ref[g], preferred_element_type=F32))
    y = jnp.concatenate(ys, axis=1)
    yt = lax.dot_general(y.astype(BF16), perm, (((1,), (1,)), ((), ())),
                         preferred_element_type=F32)
    for t in range(S5_T):
        y_ref[pl.ds(t, rows, stride=S5_T), :] = yt[:, t * LANE:(t + 1) * LANE]


def _s5_scan(p, col0, m_all, v_all, w_all, pw_all, *, n_ctx_chunks):
    m = p.shape[0]
    cb0 = col0 // LANE
    spec = pl.BlockSpec((S5_GB, S5_ROW, S5_ROW), lambda i: (i, 0, 0))
    n = S5_T * LANE
    return pl.pallas_call(
        functools.partial(_s5_kernel, n_ctx_chunks=n_ctx_chunks),
        grid=(S5_WIDTH // LANE,),
        in_specs=[pl.BlockSpec((m, LANE), lambda i: (0, cb0 + i)),
                  pl.BlockSpec((n, n), lambda i: (0, 0)), spec, spec, spec,
                  pl.BlockSpec((S5_GB, 4, 10, LANE), lambda i: (i, 0, 0, 0))],
        out_specs=pl.BlockSpec((m, LANE), lambda i: (0, i)),
        out_shape=jax.ShapeDtypeStruct((m, S5_WIDTH), F32),
        compiler_params=_cparams(("arbitrary",)),
        name="s5_scan",
    )(p, _s5_regroup_matrix(), m_all, v_all, w_all, pw_all)


def _glu_kernel(y_ref, w_ref, b_ref, o_ref):
    g = jax.nn.gelu(y_ref[...])
    gate = jnp.dot(g.astype(BF16), w_ref[...], preferred_element_type=F32) + b_ref[...]
    o_ref[...] = (g * jax.nn.sigmoid(gate)).astype(o_ref.dtype)


def _s5_glu(y, w_glu, b_glu, *, tm):
    m, w = y.shape
    return pl.pallas_call(
        _glu_kernel,
        grid=(m // tm,),
        in_specs=[pl.BlockSpec((tm, w), lambda i: (i, 0)),
                  pl.BlockSpec((w, w), lambda i: (0, 0)),
                  pl.BlockSpec((1, w), lambda i: (0, 0))],
        out_specs=pl.BlockSpec((tm, w), lambda i: (i, 0)),
        out_shape=jax.ShapeDtypeStruct((m, w), BF16),
        compiler_params=_cparams(("arbitrary",)),
        name="s5_glu",
    )(y, w_glu.astype(BF16), b_glu.reshape(1, w).astype(F32))


CONV_HALO = 8


def _conv_kernel(x_ref, w_ref, b_ref, o_ref, pad_scr, *, n_ctx, tile):
    m = x_ref.shape[0]
    half = SSD_CONV // 2
    zeros = jnp.zeros((CONV_HALO, LANE), F32)
    segs = ((0, n_ctx, CONV_HALO), (n_ctx, m, 2 * CONV_HALO))
    pad_scr[0:CONV_HALO, :] = zeros
    pad_scr[CONV_HALO + n_ctx:2 * CONV_HALO + n_ctx, :] = zeros
    pad_scr[2 * CONV_HALO + m:3 * CONV_HALO + m, :] = zeros
    for lo, hi, off in segs:
        pad_scr[lo + off:hi + off, :] = x_ref[lo:hi, :]
    bias = b_ref[...]
    for lo, hi, off in segs:
        for r0 in range(lo, hi, tile):
            acc = bias
            for tap in range(SSD_CONV):
                acc = acc + w_ref[tap:tap + 1, :] * pad_scr[r0 + off + tap - half:r0 + off + tap - half + tile, :]
            o_ref[r0:r0 + tile, :] = acc * jax.nn.sigmoid(acc)


def _conv_silu(p, conv_w, conv_b, *, n_ctx, col0):
    m = p.shape[0]
    cb0 = col0 // LANE
    return pl.pallas_call(
        functools.partial(_conv_kernel, n_ctx=n_ctx, tile=256),
        grid=(SSD_CONV_CH // LANE,),
        in_specs=[pl.BlockSpec((m, LANE), lambda j: (0, cb0 + j)),
                  pl.BlockSpec((SSD_CONV, LANE), lambda j: (0, j)),
                  pl.BlockSpec((1, LANE), lambda j: (0, j))],
        out_specs=pl.BlockSpec((m, LANE), lambda j: (0, j)),
        out_shape=jax.ShapeDtypeStruct((m, SSD_CONV_CH), F32),
        scratch_shapes=[pltpu.VMEM((m + 3 * CONV_HALO, LANE), F32)],
        compiler_params=_cparams(("arbitrary",)),
        name="conv_silu",
    )(p, conv_w, conv_b.reshape(1, SSD_CONV_CH))


def _split_dot(x, e2):
    hi = x.astype(BF16)
    lo = (x - hi.astype(F32)).astype(BF16)
    return jnp.dot(jnp.concatenate([hi, lo], axis=1), e2, preferred_element_type=F32)


def _split3(x):
    x1 = x.astype(BF16)
    r1 = x - x1.astype(F32)
    x2 = r1.astype(BF16)
    x3 = (r1 - x2.astype(F32)).astype(BF16)
    return x1, x2, x3


def _ssd_kernel(x_ref, b_ref, c_ref, dt_ref, bias_ref, a_ref, tri_ref, e_ref, y_ref, state_scr):
    t = x_ref.shape[0]
    backward = pl.program_id(0) == 1

    @pl.when(pl.program_id(1) == 0)
    def _():
        state_scr[...] = jnp.zeros(state_scr.shape, F32)

    tri = tri_ref[0]
    expand = e_ref[...]
    dt = jax.nn.softplus(dt_ref[...] + bias_ref[...])
    a = dt * a_ref[...]
    dt = jnp.where(backward, pltpu.roll(dt, SSD_HEADS, axis=1), dt)
    a = jnp.where(backward, pltpu.roll(a, SSD_HEADS, axis=1), a)
    at = a.T
    cs = sum(jnp.dot(tri, ai, preferred_element_type=F32) for ai in _split3(a))
    cst = sum(lax.dot_general(ai, tri, (((1,), (1,)), ((), ())), preferred_element_type=F32)
              for ai in _split3(at))
    total = jnp.sum(a, axis=0, keepdims=True)
    dt_x = _split_dot(dt, expand)
    din_x = _split_dot(jnp.exp(total - cs), expand)
    dout_x = _split_dot(jnp.exp(cs), expand)
    cdec_x = _split_dot(jnp.broadcast_to(jnp.exp(total), (8, LANE)), expand)[0:1]

    xdt = x_ref[...] * dt_x
    xin = (xdt * din_x).astype(BF16)
    xdt = xdt.astype(BF16)
    lane_head = lax.broadcasted_iota(jnp.int32, (1, LANE), 1) // SSD_HEAD_DIM
    gw = SSD_HPG * SSD_HEAD_DIM
    for g in range(SSD_GROUPS):
        bf = b_ref[:, g * SSD_STATE:(g + 1) * SSD_STATE]
        bg = bf.astype(BF16)
        bgt = bf.T.astype(BF16)
        cg = c_ref[:, g * SSD_STATE:(g + 1) * SSD_STATE].astype(BF16)
        cb = lax.dot_general(cg, bg, (((1,), (1,)), ((), ())), preferred_element_type=F32)
        gs = slice(g * gw, (g + 1) * gw)
        st = state_scr[:, gs]
        y_off = jnp.dot(cg, st.astype(BF16), preferred_element_type=F32) * dout_x[:, gs]
        state_scr[:, gs] = st * cdec_x[:, gs] + jnp.dot(bgt, xin[:, gs], preferred_element_type=F32)
        for pr in range(SSD_HPG // 2):
            cols = slice(g * gw + pr * LANE, g * gw + (pr + 1) * LANE)
            xp = xdt[:, cols]
            mms, xhs = [], []
            for sub in range(2):
                h = g * SSD_HPG + pr * 2 + sub
                seg = jnp.exp(cs[:, h:h + 1] - cst[h:h + 1, :])
                mms.append(jnp.where(tri > 0, cb * seg, 0.0).astype(BF16))
                xhs.append(jnp.where(lane_head == sub, xp, jnp.zeros((), BF16)))
            yd = jnp.dot(jnp.concatenate(mms, axis=1), jnp.concatenate(xhs, axis=0),
                         preferred_element_type=F32)
            y_ref[0, :, cols] = yd + y_off[:, pr * LANE:(pr + 1) * LANE]


def _ssd_scan(xbc, dt_raw, bias, a_neg, *, n_ctx):
    m = xbc.shape[0]
    t = SSD_CHUNK
    nchunks = m // t
    ctx_chunks = n_ctx // t

    def pos(d, c):
        back = jnp.where(c < ctx_chunks, ctx_chunks - 1 - c, nchunks - 1 + ctx_chunks - c)
        return jnp.where(d == 0, c, back)

    idx = jnp.arange(t)
    tri = jnp.stack([idx[:, None] >= idx[None, :], idx[:, None] <= idx[None, :]]).astype(BF16)
    expand = (jnp.arange(LANE)[:, None] == (jnp.arange(SSD_INNER)[None, :] // SSD_HEAD_DIM)).astype(BF16)
    expand = jnp.concatenate([expand, expand], axis=0)
    xb = SSD_INNER // 1024
    return pl.pallas_call(
        _ssd_kernel,
        grid=(2, nchunks),
        in_specs=[pl.BlockSpec((t, SSD_INNER), lambda d, c: (pos(d, c), 0)),
                  pl.BlockSpec((t, 1024), lambda d, c: (pos(d, c), xb)),
                  pl.BlockSpec((t, 1024), lambda d, c: (pos(d, c), xb + 1)),
                  pl.BlockSpec((t, LANE), lambda d, c: (pos(d, c), 0)),
                  pl.BlockSpec((1, LANE), lambda d, c: (0, 0)),
                  pl.BlockSpec((1, LANE), lambda d, c: (0, 0)),
                  pl.BlockSpec((1, t, t), lambda d, c: (d, 0, 0)),
                  pl.BlockSpec((2 * LANE, SSD_INNER), lambda d, c: (0, 0))],
        out_specs=pl.BlockSpec((1, t, SSD_INNER), lambda d, c: (d, pos(d, c), 0)),
        out_shape=jax.ShapeDtypeStruct((2, m, SSD_INNER), F32),
        scratch_shapes=[pltpu.VMEM((SSD_STATE, SSD_INNER), F32)],
        compiler_params=_cparams(("arbitrary", "arbitrary")),
        name="ssd_scan",
    )(xbc, xbc, xbc, dt_raw, bias.reshape(1, LANE), a_neg.reshape(1, LANE), tri, expand)


def _ssd_finish_kernel(y_ref, x_ref, z_ref, d_ref, nw_ref, o_ref):
    z = z_ref[...]
    y = (x_ref[...] * d_ref[...] + y_ref[0] + y_ref[1]) * (z * jax.nn.sigmoid(z))
    o_ref[...] = _rms_rows(y, nw_ref[...]).astype(o_ref.dtype)


def _ssd_finish(y2, xbc, p, d_x, norm_w, *, tm):
    m = xbc.shape[0]
    w = SSD_INNER
    return pl.pallas_call(
        _ssd_finish_kernel,
        grid=(m // tm,),
        in_specs=[pl.BlockSpec((2, tm, w), lambda i: (0, i, 0)),
                  pl.BlockSpec((tm, w), lambda i: (i, 0)),
                  pl.BlockSpec((tm, w), lambda i: (i, 0)),
                  pl.BlockSpec((1, w), lambda i: (0, 0)),
                  pl.BlockSpec((1, w), lambda i: (0, 0))],
        out_specs=pl.BlockSpec((tm, w), lambda i: (i, 0)),
        out_shape=jax.ShapeDtypeStruct((m, w), BF16),
        compiler_params=_cparams(("arbitrary",)),
        name="ssd_finish",
    )(y2, xbc, p, d_x, norm_w.reshape(1, w))


def _even_mixer(hcat, g0, shift, scale, w_in, j, lam_p, subln, s5p, cos, sin, lam_init, *, n_ctx, tm):
    m = hcat.shape[0]
    p = _normmod_matmul(hcat, g0, shift, scale, w_in, j, n_ctx=n_ctx, tm=tm, tn=512,
                        name="in_proj_even")
    q, k, vt = _rope_split(p, cos, sin, tm=_tile(m, 256, LANE))
    lp = lam_p.astype(F32)
    lam = (jnp.exp(jnp.sum(lp[0] * lp[1])) - jnp.exp(jnp.sum(lp[2] * lp[3])) + lam_init).reshape(1, 1)
    tq = _tile(math.gcd(n_ctx, m - n_ctx), 256, LANE)
    o_lat = _diff_attention(lam, q, k, vt, subln, q_row0=n_ctx, n_q=m - n_ctx, n_k=m, tq=tq,
                            tk=_tile(m, 768, LANE), lam_init=lam_init, name="diff_attn")
    o_ctx = _diff_attention(lam, q, k, vt, subln, q_row0=0, n_q=n_ctx, n_k=n_ctx, tq=tq,
                            tk=_tile(n_ctx, 768, LANE), lam_init=lam_init, name="diff_attn_ctx")
    y = _s5_scan(p, 2 * DIFF_QK_W + DIFF_V_W, *_s5_operators(*s5p[:8]), n_ctx_chunks=n_ctx // S5_T)
    s = _s5_glu(y, s5p[8], s5p[9], tm=_tile(m, 1056, 16))
    return jnp.concatenate([jnp.concatenate([o_ctx, o_lat], axis=0), s], axis=-1)


def _odd_mixer(hcat, g0, shift, scale, w_in, j, conv_w, conv_b, dt_bias, a_log, d_skip, norm_w, *, n_ctx, tm):
    p, dt_raw = _normmod_matmul(hcat, g0, shift, scale, w_in, j, n_ctx=n_ctx, tm=tm, tn=512,
                                name="in_proj_odd", n_main=SSD_INNER + SSD_CONV_CH)
    xbc = _conv_silu(p, conv_w, conv_b, n_ctx=n_ctx, col0=SSD_INNER)
    a_neg = -jnp.exp(a_log.astype(F32))
    y2 = _ssd_scan(xbc, dt_raw, dt_bias.astype(F32), a_neg, n_ctx=n_ctx)
    d_x = jnp.repeat(d_skip.astype(F32), SSD_HEAD_DIM).reshape(1, SSD_INNER)
    return _ssd_finish(y2, xbc, p, d_x, norm_w, tm=_tile(xbc.shape[0], 176, 16))


def kernel(x, c, ctx, c_ctx, w_mod, b_mod, norm_g, w_in_even, w_out_even, diff_lam, diff_subln, s5_lam_re, s5_lam_im, s5_log_dt, s5_b_re, s5_b_im, s5_c_re, s5_c_im, s5_d, s5_w_glu, s5_b_glu, w_in_odd, conv_w, conv_b, ssd_dt_bias, ssd_a_log, ssd_d, ssd_norm_w, w_out_odd, w_up, w_down):
    bsz, seq, d = x.shape
    assert bsz == 1 and d == D_MODEL
    n_ctx = ctx.shape[1]
    hcat = jnp.concatenate([ctx[0], x[0]], axis=0)
    m = hcat.shape[0]
    tm = _tile(m, 2112, 16)
    tm_acc = _tile(m, 1056, 16)
    mods = _modulation(c_ctx, c[0], w_mod, b_mod).reshape(DEPTH, 2, N_MOD, d)
    cos, sin = _rope_tables(n_ctx, seq)
    for i in range(DEPTH):
        md = mods[i]
        g = norm_g[i]
        j = i // 2
        if i % 2 == 0:
            lam_init = 0.8 - 0.6 * math.exp(-0.3 * i)
            s5p = (s5_lam_re[j], s5_lam_im[j], s5_log_dt[j], s5_b_re[j], s5_b_im[j], s5_c_re[j], s5_c_im[j],
                   s5_d[j], s5_w_glu[j], s5_b_glu[j])
            mix = _even_mixer(hcat, g[0], md[:, 0], md[:, 1], w_in_even, j, diff_lam[j],
                              diff_subln[j], s5p, cos, sin, lam_init, n_ctx=n_ctx, tm=tm)
            w_out = w_out_even
        else:
            mix = _odd_mixer(hcat, g[0], md[:, 0], md[:, 1], w_in_odd, j, conv_w[j], conv_b[j], ssd_dt_bias[j],
                             ssd_a_log[j], ssd_d[j], ssd_norm_w[j], n_ctx=n_ctx, tm=tm)
            w_out = w_out_odd
        hcat = _matmul_resid(mix, w_out, j, hcat, g[1], md[:, 2], n_ctx=n_ctx, tm=tm_acc, tk=512, name="out_proj")
        hcat = _mlp_resid(hcat, g[2], md[:, 3], md[:, 4], w_up, w_down, i, g[3], md[:, 5],
                          n_ctx=n_ctx, tm=tm_acc, tn=512)
    return hcat[n_ctx:][None]
```

```python
import functools
import math

import jax
import jax.numpy as jnp
from jax import lax
from jax.experimental import pallas as pl
from jax.experimental.pallas import tpu as pltpu

F32 = jnp.float32
BF16 = jnp.bfloat16

D_MODEL = 2048
DEPTH = 4
GRID_W = 64
N_MOD = 6
RMS_EPS = 1e-6

DIFF_HEADS = 8
DIFF_QK_DIM = 64
DIFF_V_DIM = 128
DIFF_QK_W = 1024
DIFF_V_W = 1024
ROPE_BASE = 10000.0
ROPE_AXIS_DIM = 32
LOG2_E = math.log2(math.e)
VT_ROWS = DIFF_V_DIM + 16

S5_WIDTH = 1024
S5_P = 16
S5_GROUPS = 64
S5_STATE = 64
S5_T = 16
S5_ROW = S5_T * S5_P
S5_GB = 128 // S5_P

SSD_INNER = 4096
SSD_HEAD_DIM = 64
SSD_HEADS = 64
SSD_GROUPS = 8
SSD_HPG = 8
SSD_STATE = 128
SSD_CONV = 5
SSD_CHUNK = 128
SSD_CONV_CH = 6144
ODD_IN_W = 10368

MLP_HIDDEN = 8192

LANE = 128
VMEM_LIMIT = 56 * 1024 * 1024


def _tile(n, target, mult):
    best = None
    for cand in range(mult, min(n, target) + 1, mult):
        if n % cand == 0:
            best = cand
    assert best is not None, (n, target, mult)
    return best


def _cparams(sem):
    return pltpu.CompilerParams(dimension_semantics=sem, vmem_limit_bytes=VMEM_LIMIT)


def _rms_rows(x, g):
    return x * lax.rsqrt(jnp.mean(x * x, axis=-1, keepdims=True) + RMS_EPS) * g


def _pick_rows(sel_ctx, ref):
    return jnp.where(sel_ctx, ref[0:1, :], ref[1:2, :])


ROW_CHUNK = 64


def _for_row_chunks(n_rows, row0, ctx_rows, fn):
    rc = _tile(n_rows, ROW_CHUNK, 16)

    def body(r, carry):
        start = pl.multiple_of(r * rc, rc)
        is_ctx = (row0 + start + lax.broadcasted_iota(jnp.int32, (rc, 1), 0)) < ctx_rows
        fn(pl.ds(start, rc), is_ctx)
        return carry

    lax.fori_loop(0, n_rows // rc, body, 0)


def _normmod_store(h_ref, g_ref, shift_ref, scale_ref, a_scr, *, row0, ctx_rows):
    def fn(rows, is_ctx):
        y = _rms_rows(h_ref[rows, :], g_ref[...])
        a_scr[rows, :] = (y * (1.0 + _pick_rows(is_ctx, scale_ref)) + _pick_rows(is_ctx, shift_ref)).astype(BF16)

    _for_row_chunks(h_ref.shape[0], row0, ctx_rows, fn)


def _gated_resid_store(o_ref, h_ref, g_ref, gate_ref, *, row0, ctx_rows):
    def fn(rows, is_ctx):
        o_ref[rows, :] = h_ref[rows, :] + _pick_rows(is_ctx, gate_ref) * _rms_rows(o_ref[rows, :], g_ref[...])

    _for_row_chunks(o_ref.shape[0], row0, ctx_rows, fn)


def _mod_kernel(c_ref, w_ref, b_ref, o_ref):
    a = c_ref[...]
    a = a * jax.nn.sigmoid(a)
    for n0 in range(0, w_ref.shape[2], LANE):
        cols = slice(n0, n0 + LANE)
        w = w_ref[0, :, cols]
        rows = [jnp.sum(w * a[r], axis=0, keepdims=True) for r in range(2)]
        o_ref[0, :, cols] = jnp.concatenate(rows, axis=0) + b_ref[0, :, cols]


def _modulation(c_ctx, c, w_mod, b_mod):
    depth, d, n = w_mod.shape
    tn = _tile(n, 2048, LANE)
    c_lanes = jnp.broadcast_to(jnp.stack([c_ctx, c])[:, :, None], (2, d, LANE)).astype(F32)
    return pl.pallas_call(
        _mod_kernel,
        grid=(depth, n // tn),
        in_specs=[pl.BlockSpec((2, d, LANE), lambda l, j: (0, 0, 0)),
                  pl.BlockSpec((1, d, tn), lambda l, j: (l, 0, j)),
                  pl.BlockSpec((1, 1, tn), lambda l, j: (l, 0, j))],
        out_specs=pl.BlockSpec((1, 2, tn), lambda l, j: (l, 0, j)),
        out_shape=jax.ShapeDtypeStruct((depth, 2, n), F32),
        compiler_params=_cparams(("arbitrary", "arbitrary")),
        name="modulation",
    )(c_lanes, w_mod, b_mod.reshape(depth, 1, n))


def _normmod_matmul_kernel(h_ref, g_ref, shift_ref, scale_ref, w_ref, *rest, ctx_rows, row_step, tail):
    if tail:
        wt_ref, o_ref, ot_ref, a_scr = rest
    else:
        o_ref, a_scr = rest

    @pl.when(pl.program_id(1) == 0)
    def _():
        _normmod_store(h_ref, g_ref, shift_ref, scale_ref, a_scr,
                       row0=pl.program_id(0) * row_step, ctx_rows=ctx_rows)
        if tail:
            ot_ref[...] = jnp.dot(a_scr[...], wt_ref[...].astype(BF16), preferred_element_type=F32)

    o_ref[...] = jnp.dot(a_scr[...], w_ref[...].astype(BF16), preferred_element_type=F32)


ACC_COLS = 512


def _accumulate(o_ref, lhs, w_ref, k):
    @pl.when(k == 0)
    def _():
        o_ref[...] = jnp.zeros(o_ref.shape, F32)

    for n0 in range(0, o_ref.shape[1], ACC_COLS):
        cols = slice(n0, n0 + ACC_COLS)
        o_ref[:, cols] += jnp.dot(lhs, w_ref[:, cols].astype(BF16), preferred_element_type=F32)


def _normmod_matmul(h, g, shift, scale, w, layer, *, n_ctx, tm, tn, name, n_main=None):
    m, d = h.shape
    n = w.shape[2]
    n_main = n if n_main is None else n_main
    n_tail = n - n_main
    assert n_main % tn == 0 and (n_tail == 0 or n_main % n_tail == 0)
    in_specs = [pl.BlockSpec((tm, d), lambda i, j: (i, 0), pipeline_mode=pl.Buffered(1)),
                pl.BlockSpec((1, d), lambda i, j: (0, 0)),
                pl.BlockSpec((2, d), lambda i, j: (0, 0)),
                pl.BlockSpec((2, d), lambda i, j: (0, 0)),
                pl.BlockSpec((None, d, tn), lambda i, j: (layer, 0, j))]
    out_specs = [pl.BlockSpec((tm, tn), lambda i, j: (i, j))]
    out_shape = [jax.ShapeDtypeStruct((m, n_main), F32)]
    args = [h, g.reshape(1, d), shift, scale, w]
    if n_tail:
        in_specs.append(pl.BlockSpec((None, d, n_tail), lambda i, j: (layer, 0, n_main // n_tail)))
        out_specs.append(pl.BlockSpec((tm, n_tail), lambda i, j: (i, 0)))
        out_shape.append(jax.ShapeDtypeStruct((m, n_tail), F32))
        args.append(w)
    outs = pl.pallas_call(
        functools.partial(_normmod_matmul_kernel, ctx_rows=n_ctx, row_step=tm, tail=bool(n_tail)),
        grid=(m // tm, n_main // tn),
        in_specs=in_specs,
        out_specs=out_specs,
        out_shape=out_shape,
        scratch_shapes=[pltpu.VMEM((tm, d), BF16)],
        compiler_params=_cparams(("arbitrary", "arbitrary")),
        name=name,
    )(*args)
    return outs if n_tail else outs[0]


def _matmul_resid_kernel(a_ref, w_ref, h_ref, g_ref, gate_ref, o_ref, *, n_ctx, tm, nk):
    k = pl.program_id(1)
    _accumulate(o_ref, a_ref[...], w_ref, k)

    @pl.when(k == nk - 1)
    def _():
        _gated_resid_store(o_ref, h_ref, g_ref, gate_ref, row0=pl.program_id(0) * tm, ctx_rows=n_ctx)


def _matmul_resid(a, w, layer, h, g, gate, *, n_ctx, tm, tk, name):
    m, kdim = a.shape
    d = w.shape[2]
    nk = kdim // tk
    return pl.pallas_call(
        functools.partial(_matmul_resid_kernel, n_ctx=n_ctx, tm=tm, nk=nk),
        grid=(m // tm, nk),
        in_specs=[pl.BlockSpec((tm, tk), lambda i, k: (i, k)),
                  pl.BlockSpec((None, tk, d), lambda i, k: (layer, k, 0)),
                  pl.BlockSpec((tm, d), lambda i, k: (i, 0)),
                  pl.BlockSpec((1, d), lambda i, k: (0, 0)),
                  pl.BlockSpec((2, d), lambda i, k: (0, 0))],
        out_specs=pl.BlockSpec((tm, d), lambda i, k: (i, 0)),
        out_shape=jax.ShapeDtypeStruct((m, d), F32),
        compiler_params=_cparams(("arbitrary", "arbitrary")),
        name=name,
    )(a, w, h, g.reshape(1, d), gate)


def _mlp_kernel(h_ref, g_in_ref, shift_ref, scale_ref, wup_ref, wdn_ref, g_out_ref, gate_ref, o_ref, a_scr,
                *, n_ctx, tm, nj):
    row0 = pl.program_id(0) * tm
    j = pl.program_id(1)

    @pl.when(j == 0)
    def _():
        _normmod_store(h_ref, g_in_ref, shift_ref, scale_ref, a_scr, row0=row0, ctx_rows=n_ctx)

    hid = jnp.dot(a_scr[...], wup_ref[...].astype(BF16), preferred_element_type=F32)
    hid = jnp.square(jnp.maximum(hid, 0.0)).astype(BF16)
    _accumulate(o_ref, hid, wdn_ref, j)

    @pl.when(j == nj - 1)
    def _():
        _gated_resid_store(o_ref, h_ref, g_out_ref, gate_ref, row0=row0, ctx_rows=n_ctx)


def _mlp_resid(h, g_in, shift, scale, w_up, w_down, layer, g_out, gate, *, n_ctx, tm, tn):
    m, d = h.shape
    hidden = w_up.shape[2]
    nj = hidden // tn
    row = lambda i, j: (i, 0)
    fixed = lambda i, j: (0, 0)
    return pl.pallas_call(
        functools.partial(_mlp_kernel, n_ctx=n_ctx, tm=tm, nj=nj),
        grid=(m // tm, nj),
        in_specs=[pl.BlockSpec((tm, d), row, pipeline_mode=pl.Buffered(1)),
                  pl.BlockSpec((1, d), fixed),
                  pl.BlockSpec((2, d), fixed),
                  pl.BlockSpec((2, d), fixed),
                  pl.BlockSpec((None, d, tn), lambda i, j: (layer, 0, j)),
                  pl.BlockSpec((None, tn, d), lambda i, j: (layer, j, 0)),
                  pl.BlockSpec((1, d), fixed),
                  pl.BlockSpec((2, d), fixed)],
        out_specs=pl.BlockSpec((tm, d), row, pipeline_mode=pl.Buffered(1)),
        out_shape=jax.ShapeDtypeStruct((m, d), F32),
        scratch_shapes=[pltpu.VMEM((tm, d), BF16)],
        compiler_params=_cparams(("arbitrary", "arbitrary")),
        name="mlp",
    )(h, g_in.reshape(1, d), shift, scale, w_up, w_down, g_out.reshape(1, d), gate)


def _rope_tables(n_ctx, seq_len):
    rows = seq_len // GRID_W
    row = jnp.repeat(jnp.arange(rows, dtype=F32), GRID_W)
    col = jnp.tile(jnp.arange(GRID_W, dtype=F32), rows)
    inv = ROPE_BASE ** (-jnp.arange(0, ROPE_AXIS_DIM, 2, dtype=F32) / ROPE_AXIS_DIM)
    ang_r = row[:, None] * inv
    ang_c = col[:, None] * inv
    ang = jnp.concatenate([ang_r, ang_r, ang_c, ang_c], axis=-1)
    cos = jnp.concatenate([jnp.ones((n_ctx, DIFF_QK_DIM), F32), jnp.cos(ang)], axis=0)
    sin = jnp.concatenate([jnp.zeros((n_ctx, DIFF_QK_DIM), F32), jnp.sin(ang)], axis=0)
    sign = jnp.where((jnp.arange(DIFF_QK_DIM) % 32) < 16, -1.0, 1.0).astype(F32)
    sin = sin * sign
    return jnp.tile(cos, (1, 2)), jnp.tile(sin, (1, 2))


def _rope_kernel(p_ref, cos_ref, sin_ref, q_ref, k_ref, vt_ref):
    cos = cos_ref[...]
    sin = sin_ref[...]
    low = (lax.broadcasted_iota(jnp.int32, (1, LANE), 1) % 32) < 16

    def rope(x):
        rot = jnp.where(low, pltpu.roll(x, LANE - 16, axis=1), pltpu.roll(x, 16, axis=1))
        return x * cos + rot * sin

    for c in range(DIFF_QK_W // LANE):
        sl = slice(c * LANE, (c + 1) * LANE)
        q_ref[:, sl] = (rope(p_ref[:, sl]) * (DIFF_QK_DIM ** -0.5 * LOG2_E)).astype(BF16)
        k_ref[:, sl] = rope(p_ref[:, DIFF_QK_W + c * LANE:DIFF_QK_W + (c + 1) * LANE]).astype(BF16)
    for h in range(DIFF_HEADS):
        v = p_ref[:, 2 * DIFF_QK_W + h * DIFF_V_DIM:2 * DIFF_QK_W + (h + 1) * DIFF_V_DIM]
        vt_ref[h, 0:DIFF_V_DIM, :] = v.T.astype(BF16)
        vt_ref[h, DIFF_V_DIM:VT_ROWS, :] = jnp.ones((VT_ROWS - DIFF_V_DIM, v.shape[0]), BF16)


def _rope_split(p, cos, sin, *, tm):
    m = p.shape[0]
    w = 2 * DIFF_QK_W + DIFF_V_W
    out = jax.ShapeDtypeStruct((m, DIFF_QK_W), BF16)
    return pl.pallas_call(
        _rope_kernel,
        grid=(m // tm,),
        in_specs=[pl.BlockSpec((tm, w), lambda i: (i, 0)),
                  pl.BlockSpec((tm, LANE), lambda i: (i, 0)),
                  pl.BlockSpec((tm, LANE), lambda i: (i, 0))],
        out_specs=[pl.BlockSpec((tm, DIFF_QK_W), lambda i: (i, 0)),
                   pl.BlockSpec((tm, DIFF_QK_W), lambda i: (i, 0)),
                   pl.BlockSpec((DIFF_HEADS, VT_ROWS, tm), lambda i: (0, 0, i))],
        out_shape=[out, out, jax.ShapeDtypeStruct((DIFF_HEADS, VT_ROWS, m), BF16)],
        compiler_params=_cparams(("arbitrary",)),
        name="rope_split",
    )(p, cos, sin)


def _attn_kernel(lam_ref, q0_ref, q1_ref, k0_ref, k1_ref, vt_ref, subln_ref, o_ref, acc_scr, s_scr,
                 *, tk, nk, out_scale):
    tq = q0_ref.shape[0]
    lam = lam_ref[0, 0]
    lane_head = lax.broadcasted_iota(jnp.int32, (1, LANE), 1) // DIFF_QK_DIM
    nt = (((1,), (1,)), ((), ()))
    k_refs = (k0_ref, k1_ref)

    def keys(kb):
        start = kb * tk
        return pl.ds(start if isinstance(kb, int) else pl.multiple_of(start, tk), tk)

    for sub in range(2):
        qs = [jnp.where(lane_head == sub, q_ref[...], jnp.zeros((), BF16)) for q_ref in (q0_ref, q1_ref)]
        acc_scr[...] = jnp.zeros(acc_scr.shape, F32)

        def scores(kb, slot):
            for mp in range(2):
                s_scr[slot, mp] = lax.dot_general(k_refs[mp][keys(kb), :], qs[mp], nt,
                                                  preferred_element_type=F32)

        def softmax_pv(kb, slot, ms):
            vt = vt_ref[sub, :, keys(kb)]
            new_ms = []
            for mp in range(2):
                s = s_scr[slot, mp]
                m_new = jnp.maximum(ms[mp], jnp.max(s, axis=0, keepdims=True))
                alpha = jnp.exp2(ms[mp] - m_new)
                p = jnp.exp2((s - m_new).astype(BF16))
                acc_scr[mp] = alpha * acc_scr[mp] + jnp.dot(vt, p, preferred_element_type=F32)
                new_ms.append(m_new)
            return tuple(new_ms)

        def pair(i, ms):
            scores(2 * i + 1, 1)
            ms = softmax_pv(2 * i, 0, ms)
            scores(2 * i + 2, 0)
            return softmax_pv(2 * i + 1, 1, ms)

        m_init = jnp.full((1, tq), -jnp.inf, F32)
        scores(0, 0)
        ms = lax.fori_loop(0, (nk - 1) // 2, pair, (m_init, m_init))
        if (nk - 1) % 2 == 1:
            scores(nk - 1, 1)
            ms = softmax_pv(nk - 2, 0, ms)
            softmax_pv(nk - 1, 1, ms)
        else:
            softmax_pv(nk - 1, 0, ms)
        o0 = acc_scr[0, 0:DIFF_V_DIM, :] / acc_scr[0, DIFF_V_DIM:DIFF_V_DIM + 1, :]
        o1 = acc_scr[1, 0:DIFF_V_DIM, :] / acc_scr[1, DIFF_V_DIM:DIFF_V_DIM + 1, :]
        o = o0 - lam * o1
        o = o * lax.rsqrt(jnp.mean(o * o, axis=0, keepdims=True) + RMS_EPS) * (subln_ref[...] * out_scale)
        o_ref[:, sub * DIFF_V_DIM:(sub + 1) * DIFF_V_DIM] = o.T.astype(o_ref.dtype)


def _diff_attention(lam, q, k, vt, subln, *, q_row0, n_q, n_k, tq, tk, lam_init, name):
    pairs = DIFF_HEADS // 2
    assert q_row0 % tq == 0 and n_q % tq == 0
    qb0 = q_row0 // tq
    nk = n_k // tk
    return pl.pallas_call(
        functools.partial(_attn_kernel, tk=tk, nk=nk, out_scale=1.0 - lam_init),
        grid=(pairs, n_q // tq),
        in_specs=[pl.BlockSpec(memory_space=pltpu.SMEM),
                  pl.BlockSpec((tq, LANE), lambda j, i: (qb0 + i, j)),
                  pl.BlockSpec((tq, LANE), lambda j, i: (qb0 + i, pairs + j)),
                  pl.BlockSpec((n_k, LANE), lambda j, i: (0, j)),
                  pl.BlockSpec((n_k, LANE), lambda j, i: (0, pairs + j)),
                  pl.BlockSpec((2, VT_ROWS, n_k), lambda j, i: (j, 0, 0)),
                  pl.BlockSpec((DIFF_V_DIM, 1), lambda j, i: (0, 0))],
        out_specs=pl.BlockSpec((tq, 2 * DIFF_V_DIM), lambda j, i: (i, j)),
        out_shape=jax.ShapeDtypeStruct((n_q, DIFF_V_W), BF16),
        scratch_shapes=[pltpu.VMEM((2, VT_ROWS, tq), F32), pltpu.VMEM((2, 2, tk, tq), F32)],
        compiler_params=_cparams(("arbitrary", "arbitrary")),
        name=name,
    )(lam, q, q, k, k, vt, subln.reshape(DIFF_V_DIM, 1).astype(F32))


def _s5_operators(lam_re, lam_im, log_dt, b_re, b_im, c_re, c_im, d_skip):
    t = S5_T
    steps = jnp.arange(t + 1, dtype=F32)
    ops = {}
    kerns = []
    for d in range(2):
        lam = lax.complex(lam_re[d].astype(F32), lam_im[d].astype(F32))
        dt = jnp.exp(log_dt[d].astype(F32))[:, None]
        lam_dt = lam * dt
        lam_bar = jnp.exp(lam_dt)
        b_bar = ((lam_bar - 1) / lam)[..., None] * lax.complex(b_re[d].astype(F32), b_im[d].astype(F32))
        c_mat = lax.complex(c_re[d].astype(F32), c_im[d].astype(F32))
        pw = jnp.exp(lam_dt[None] * steps[:, None, None])
        kerns.append(jnp.real(jnp.einsum('gpn,tgn,gnq->gtqp', c_mat, pw[:t], b_bar,
                                         precision=lax.Precision.HIGHEST)))
        pw_in = pw[:t][::-1] if d == 0 else pw[:t]
        vin = pw_in[:, :, :, None] * b_bar[None]
        vin = jnp.transpose(vin, (1, 0, 3, 2)).reshape(S5_GROUPS, S5_ROW, S5_STATE)
        v_d = jnp.concatenate([jnp.real(vin), jnp.imag(vin)], axis=-1)
        pw_out = pw[1:] if d == 0 else pw[1:][::-1]
        wout = c_mat[None] * pw_out[:, :, None, :]
        wout = jnp.transpose(wout, (1, 3, 0, 2)).reshape(S5_GROUPS, S5_STATE, S5_ROW)
        w_d = jnp.concatenate([jnp.real(wout), -jnp.imag(wout)], axis=1)
        kk = (2.0 ** jnp.arange(10, dtype=F32)) * t
        ak = jnp.exp(lam_dt[:, None, :] * kk[None, :, None])
        a1 = jnp.concatenate([jnp.real(ak), jnp.real(ak)], axis=-1)
        a2 = jnp.concatenate([-jnp.imag(ak), jnp.imag(ak)], axis=-1)
        ops[d] = (v_d, w_d, a1, a2)
    kf, kb = kerns
    skip = jnp.eye(S5_P, dtype=F32)[None, None] * d_skip.astype(F32)[:, None, None, :]
    k_lag = jnp.concatenate([kb[:, :0:-1], kf[:, :1] + kb[:, :1] + skip, kf[:, 1:]], axis=1)
    rows = jnp.stack([k_lag[:, t - 1 - s:2 * t - 1 - s] for s in range(t)], axis=1)
    m_all = jnp.transpose(rows, (0, 1, 3, 2, 4)).reshape(S5_GROUPS, S5_ROW, S5_ROW)
    v_all = jnp.concatenate([ops[0][0], ops[1][0]], axis=-1)
    w_all = jnp.concatenate([ops[0][1], ops[1][1]], axis=1)
    pw_all = jnp.stack([ops[0][2], ops[0][3], ops[1][2], ops[1][3]], axis=1)
    return m_all.astype(BF16), v_all.astype(BF16), w_all.astype(BF16), pw_all


def _shift_rows(x, s, up):
    n = x.shape[0]
    if s >= n:
        return jnp.zeros_like(x)
    if s % 8 == 0:
        z = jnp.zeros((s, x.shape[1]), x.dtype)
        return jnp.concatenate([x[s:], z], axis=0) if up else jnp.concatenate([z, x[:n - s]], axis=0)
    row = lax.broadcasted_iota(jnp.int32, x.shape, 0)
    if up:
        return jnp.where(row < n - s, pltpu.roll(x, n - s, axis=0), 0.0)
    return jnp.where(row >= s, pltpu.roll(x, s, axis=0), 0.0)


def _s5_chunk_scan(z, a1_ref, a2_ref, up):
    x = _shift_rows(z, 1, up)
    n = x.shape[0]
    k = 0
    while (1 << k) < n:
        xs = _shift_rows(x, 1 << k, up)
        x = x + a1_ref[k:k + 1, :] * xs + a2_ref[k:k + 1, :] * pltpu.roll(xs, S5_STATE, axis=1)
        k += 1
    return x


def _s5_regroup_matrix():
    n = S5_T * LANE
    src = jnp.arange(n)
    t, g, q = src // LANE, (src % LANE) // S5_P, src % S5_P
    dst = g * S5_ROW + t * S5_P + q
    return (dst[:, None] == jnp.arange(n)[None, :]).astype(BF16)


def _s5_kernel(x_ref, perm_ref, m_ref, v_ref, w_ref, pw_ref, y_ref, *, n_ctx_chunks):
    nc = n_ctx_chunks
    rows = x_ref.shape[0] // S5_T
    perm = perm_ref[...]
    x = jnp.concatenate([x_ref[pl.ds(t, rows, stride=S5_T), :].astype(BF16) for t in range(S5_T)],
                        axis=1)
    xp = jnp.dot(x, perm, preferred_element_type=F32).astype(BF16)
    ys = []
    for g in range(S5_GB):
        u = xp[:, g * S5_ROW:(g + 1) * S5_ROW]
        y = jnp.dot(u, m_ref[g], preferred_element_type=F32)
        z = jnp.dot(u, v_ref[g], preferred_element_type=F32)
        zf = z[:, :LANE]
        zb = z[:, LANE:]
        sf = _s5_chunk_scan(zf, pw_ref.at[g, 0], pw_ref.at[g, 1], up=False)
        zb = jnp.concatenate([zb[nc:], zb[:nc]], axis=0)
        sb = _s5_chunk_scan(zb, pw_ref.at[g, 2], pw_ref.at[g, 3], up=True)
        nl = sb.shape[0] - nc
        sb = jnp.concatenate([sb[nl:], sb[:nl]], axis=0)
        s = jnp.concatenate([sf, sb], axis=1).astype(BF16)
        ys.append(y + jnp.dot(s, w_ref[g], preferred_element_type=F32))
    y = jnp.concatenate(ys, axis=1)
    yt = lax.dot_general(y.astype(BF16), perm, (((1,), (1,)), ((), ())),
                         preferred_element_type=F32)
    for t in range(S5_T):
        y_ref[pl.ds(t, rows, stride=S5_T), :] = yt[:, t * LANE:(t + 1) * LANE]


def _s5_scan(p, col0, operators, layer, *, n_ctx_chunks):
    m_all, v_all, w_all, pw_all = operators
    m = p.shape[0]
    cb0 = col0 // LANE
    spec = pl.BlockSpec((None, S5_GB, S5_ROW, S5_ROW), lambda i: (layer, i, 0, 0))
    n = S5_T * LANE
    return pl.pallas_call(
        functools.partial(_s5_kernel, n_ctx_chunks=n_ctx_chunks),
        grid=(S5_WIDTH // LANE,),
        in_specs=[pl.BlockSpec((m, LANE), lambda i: (0, cb0 + i)),
                  pl.BlockSpec((n, n), lambda i: (0, 0)), spec, spec, spec,
                  pl.BlockSpec((None, S5_GB, 4, 10, LANE), lambda i: (layer, i, 0, 0, 0))],
        out_specs=pl.BlockSpec((m, LANE), lambda i: (0, i)),
        out_shape=jax.ShapeDtypeStruct((m, S5_WIDTH), F32),
        compiler_params=_cparams(("arbitrary",)),
        name="s5_scan",
    )(p, _s5_regroup_matrix(), m_all, v_all, w_all, pw_all)


def _glu_kernel(y_ref, w_ref, b_ref, o_ref):
    g = jax.nn.gelu(y_ref[...])
    gate = jnp.dot(g.astype(BF16), w_ref[...], preferred_element_type=F32) + b_ref[...]
    o_ref[...] = (g * jax.nn.sigmoid(gate)).astype(o_ref.dtype)


def _s5_glu(y, w_glu, b_glu, *, tm):
    m, w = y.shape
    return pl.pallas_call(
        _glu_kernel,
        grid=(m // tm,),
        in_specs=[pl.BlockSpec((tm, w), lambda i: (i, 0)),
                  pl.BlockSpec((w, w), lambda i: (0, 0)),
                  pl.BlockSpec((1, w), lambda i: (0, 0))],
        out_specs=pl.BlockSpec((tm, w), lambda i: (i, 0)),
        out_shape=jax.ShapeDtypeStruct((m, w), BF16),
        compiler_params=_cparams(("arbitrary",)),
        name="s5_glu",
    )(y, w_glu.astype(BF16), b_glu.reshape(1, w).astype(F32))


CONV_HALO = 8


def _conv_kernel(x_ref, w_ref, b_ref, o_ref, pad_scr, *, n_ctx, tile):
    m = x_ref.shape[0]
    half = SSD_CONV // 2
    zeros = jnp.zeros((CONV_HALO, LANE), F32)
    segs = ((0, n_ctx, CONV_HALO), (n_ctx, m, 2 * CONV_HALO))
    pad_scr[0:CONV_HALO, :] = zeros
    pad_scr[CONV_HALO + n_ctx:2 * CONV_HALO + n_ctx, :] = zeros
    pad_scr[2 * CONV_HALO + m:3 * CONV_HALO + m, :] = zeros
    for lo, hi, off in segs:
        pad_scr[lo + off:hi + off, :] = x_ref[lo:hi, :]
    bias = b_ref[...]
    for lo, hi, off in segs:
        for r0 in range(lo, hi, tile):
            acc = bias
            for tap in range(SSD_CONV):
                acc = acc + w_ref[tap:tap + 1, :] * pad_scr[r0 + off + tap - half:r0 + off + tap - half + tile, :]
            o_ref[r0:r0 + tile, :] = acc * jax.nn.sigmoid(acc)


def _conv_silu(p, conv_w, conv_b, *, n_ctx, col0):
    m = p.shape[0]
    cb0 = col0 // LANE
    return pl.pallas_call(
        functools.partial(_conv_kernel, n_ctx=n_ctx, tile=256),
        grid=(SSD_CONV_CH // LANE,),
        in_specs=[pl.BlockSpec((m, LANE), lambda j: (0, cb0 + j)),
                  pl.BlockSpec((SSD_CONV, LANE), lambda j: (0, j)),
                  pl.BlockSpec((1, LANE), lambda j: (0, j))],
        out_specs=pl.BlockSpec((m, LANE), lambda j: (0, j)),
        out_shape=jax.ShapeDtypeStruct((m, SSD_CONV_CH), F32),
        scratch_shapes=[pltpu.VMEM((m + 3 * CONV_HALO, LANE), F32)],
        compiler_params=_cparams(("arbitrary",)),
        name="conv_silu",
    )(p, conv_w, conv_b.reshape(1, SSD_CONV_CH))


def _split_dot(x, e2):
    hi = x.astype(BF16)
    lo = (x - hi.astype(F32)).astype(BF16)
    return jnp.dot(jnp.concatenate([hi, lo], axis=1), e2, preferred_element_type=F32)


def _split3(x):
    x1 = x.astype(BF16)
    r1 = x - x1.astype(F32)
    x2 = r1.astype(BF16)
    x3 = (r1 - x2.astype(F32)).astype(BF16)
    return x1, x2, x3


def _ssd_kernel(x_ref, b_ref, c_ref, dt_ref, bias_ref, a_ref, tri_ref, e_ref, y_ref, state_scr):
    t = x_ref.shape[0]
    backward = pl.program_id(0) == 1

    @pl.when(pl.program_id(1) == 0)
    def _():
        state_scr[...] = jnp.zeros(state_scr.shape, F32)

    tri = tri_ref[0]
    expand = e_ref[...]
    dt = jax.nn.softplus(dt_ref[...] + bias_ref[...])
    a = dt * a_ref[...]
    dt = jnp.where(backward, pltpu.roll(dt, SSD_HEADS, axis=1), dt)
    a = jnp.where(backward, pltpu.roll(a, SSD_HEADS, axis=1), a)
    at = a.T
    cs = sum(jnp.dot(tri, ai, preferred_element_type=F32) for ai in _split3(a))
    cst = sum(lax.dot_general(ai, tri, (((1,), (1,)), ((), ())), preferred_element_type=F32)
              for ai in _split3(at))
    total = jnp.sum(a, axis=0, keepdims=True)
    dt_x = _split_dot(dt, expand)
    din_x = _split_dot(jnp.exp(total - cs), expand)
    dout_x = _split_dot(jnp.exp(cs), expand)
    cdec_x = _split_dot(jnp.broadcast_to(jnp.exp(total), (8, LANE)), expand)[0:1]

    xdt = x_ref[...] * dt_x
    xin = (xdt * din_x).astype(BF16)
    xdt = xdt.astype(BF16)
    lane_head = lax.broadcasted_iota(jnp.int32, (1, LANE), 1) // SSD_HEAD_DIM
    gw = SSD_HPG * SSD_HEAD_DIM
    for g in range(SSD_GROUPS):
        bf = b_ref[:, g * SSD_STATE:(g + 1) * SSD_STATE]
        bg = bf.astype(BF16)
        bgt = bf.T.astype(BF16)
        cg = c_ref[:, g * SSD_STATE:(g + 1) * SSD_STATE].astype(BF16)
        cb = lax.dot_general(cg, bg, (((1,), (1,)), ((), ())), preferred_element_type=F32)
        gs = slice(g * gw, (g + 1) * gw)
        st = state_scr[:, gs]
        y_off = jnp.dot(cg, st.astype(BF16), preferred_element_type=F32) * dout_x[:, gs]
        state_scr[:, gs] = st * cdec_x[:, gs] + jnp.dot(bgt, xin[:, gs], preferred_element_type=F32)
        for pr in range(SSD_HPG // 2):
            cols = slice(g * gw + pr * LANE, g * gw + (pr + 1) * LANE)
            xp = xdt[:, cols]
            mms, xhs = [], []
            for sub in range(2):
                h = g * SSD_HPG + pr * 2 + sub
                seg = jnp.exp(cs[:, h:h + 1] - cst[h:h + 1, :])
                mms.append(jnp.where(tri > 0, cb * seg, 0.0).astype(BF16))
                xhs.append(jnp.where(lane_head == sub, xp, jnp.zeros((), BF16)))
            yd = jnp.dot(jnp.concatenate(mms, axis=1), jnp.concatenate(xhs, axis=0),
                         preferred_element_type=F32)
            y_ref[0, :, cols] = yd + y_off[:, pr * LANE:(pr + 1) * LANE]


def _ssd_scan(xbc, dt_raw, bias, a_neg, *, n_ctx):
    m = xbc.shape[0]
    t = SSD_CHUNK
    nchunks = m // t
    ctx_chunks = n_ctx // t

    def pos(d, c):
        back = jnp.where(c < ctx_chunks, ctx_chunks - 1 - c, nchunks - 1 + ctx_chunks - c)
        return jnp.where(d == 0, c, back)

    idx = jnp.arange(t)
    tri = jnp.stack([idx[:, None] >= idx[None, :], idx[:, None] <= idx[None, :]]).astype(BF16)
    expand = (jnp.arange(LANE)[:, None] == (jnp.arange(SSD_INNER)[None, :] // SSD_HEAD_DIM)).astype(BF16)
    expand = jnp.concatenate([expand, expand], axis=0)
    xb = SSD_INNER // 1024
    return pl.pallas_call(
        _ssd_kernel,
        grid=(2, nchunks),
        in_specs=[pl.BlockSpec((t, SSD_INNER), lambda d, c: (pos(d, c), 0)),
                  pl.BlockSpec((t, 1024), lambda d, c: (pos(d, c), xb)),
                  pl.BlockSpec((t, 1024), lambda d, c: (pos(d, c), xb + 1)),
                  pl.BlockSpec((t, LANE), lambda d, c: (pos(d, c), 0)),
                  pl.BlockSpec((1, LANE), lambda d, c: (0, 0)),
                  pl.BlockSpec((1, LANE), lambda d, c: (0, 0)),
                  pl.BlockSpec((1, t, t), lambda d, c: (d, 0, 0)),
                  pl.BlockSpec((2 * LANE, SSD_INNER), lambda d, c: (0, 0))],
        out_specs=pl.BlockSpec((1, t, SSD_INNER), lambda d, c: (d, pos(d, c), 0)),
        out_shape=jax.ShapeDtypeStruct((2, m, SSD_INNER), F32),
        scratch_shapes=[pltpu.VMEM((SSD_STATE, SSD_INNER), F32)],
        compiler_params=_cparams(("arbitrary", "arbitrary")),
        name="ssd_scan",
    )(xbc, xbc, xbc, dt_raw, bias.reshape(1, LANE), a_neg.reshape(1, LANE), tri, expand)


def _ssd_finish_kernel(y_ref, x_ref, z_ref, d_ref, nw_ref, o_ref):
    z = z_ref[...]
    y = (x_ref[...] * d_ref[...] + y_ref[0] + y_ref[1]) * (z * jax.nn.sigmoid(z))
    o_ref[...] = _rms_rows(y, nw_ref[...]).astype(o_ref.dtype)


def _ssd_finish(y2, xbc, p, d_x, norm_w, *, tm):
    m = xbc.shape[0]
    w = SSD_INNER
    return pl.pallas_call(
        _ssd_finish_kernel,
        grid=(m // tm,),
        in_specs=[pl.BlockSpec((2, tm, w), lambda i: (0, i, 0)),
                  pl.BlockSpec((tm, w), lambda i: (i, 0)),
                  pl.BlockSpec((tm, w), lambda i: (i, 0)),
                  pl.BlockSpec((1, w), lambda i: (0, 0)),
                  pl.BlockSpec((1, w), lambda i: (0, 0))],
        out_specs=pl.BlockSpec((tm, w), lambda i: (i, 0)),
        out_shape=jax.ShapeDtypeStruct((m, w), BF16),
        compiler_params=_cparams(("arbitrary",)),
        name="ssd_finish",
    )(y2, xbc, p, d_x, norm_w.reshape(1, w))


def _even_mixer(hcat, g0, shift, scale, w_in, j, lam_p, subln, s5_ops, w_glu, b_glu, cos, sin, lam_init,
                *, n_ctx, tm):
    m = hcat.shape[0]
    p = _normmod_matmul(hcat, g0, shift, scale, w_in, j, n_ctx=n_ctx, tm=tm, tn=512,
                        name="in_proj_even")
    q, k, vt = _rope_split(p, cos, sin, tm=_tile(m, 256, LANE))
    lp = lam_p.astype(F32)
    lam = (jnp.exp(jnp.sum(lp[0] * lp[1])) - jnp.exp(jnp.sum(lp[2] * lp[3])) + lam_init).reshape(1, 1)
    tq = _tile(math.gcd(n_ctx, m - n_ctx), 256, LANE)
    o_lat = _diff_attention(lam, q, k, vt, subln, q_row0=n_ctx, n_q=m - n_ctx, n_k=m, tq=tq,
                            tk=_tile(m, 768, LANE), lam_init=lam_init, name="diff_attn")
    o_ctx = _diff_attention(lam, q, k, vt, subln, q_row0=0, n_q=n_ctx, n_k=n_ctx, tq=tq,
                            tk=_tile(n_ctx, 768, LANE), lam_init=lam_init, name="diff_attn_ctx")
    y = _s5_scan(p, 2 * DIFF_QK_W + DIFF_V_W, s5_ops, j, n_ctx_chunks=n_ctx // S5_T)
    s = _s5_glu(y, w_glu, b_glu, tm=_tile(m, 1056, 16))
    return jnp.concatenate([jnp.concatenate([o_ctx, o_lat], axis=0), s], axis=-1)


def _odd_mixer(hcat, g0, shift, scale, w_in, j, conv_w, conv_b, dt_bias, a_log, d_skip, norm_w, *, n_ctx, tm):
    p, dt_raw = _normmod_matmul(hcat, g0, shift, scale, w_in, j, n_ctx=n_ctx, tm=tm, tn=512,
                                name="in_proj_odd", n_main=SSD_INNER + SSD_CONV_CH)
    xbc = _conv_silu(p, conv_w, conv_b, n_ctx=n_ctx, col0=SSD_INNER)
    a_neg = -jnp.exp(a_log.astype(F32))
    y2 = _ssd_scan(xbc, dt_raw, dt_bias.astype(F32), a_neg, n_ctx=n_ctx)
    d_x = jnp.repeat(d_skip.astype(F32), SSD_HEAD_DIM).reshape(1, SSD_INNER)
    return _ssd_finish(y2, xbc, p, d_x, norm_w, tm=_tile(xbc.shape[0], 176, 16))


def kernel(x, c, ctx, c_ctx, w_mod, b_mod, norm_g, w_in_even, w_out_even, diff_lam, diff_subln, s5_lam_re, s5_lam_im, s5_log_dt, s5_b_re, s5_b_im, s5_c_re, s5_c_im, s5_d, s5_w_glu, s5_b_glu, w_in_odd, conv_w, conv_b, ssd_dt_bias, ssd_a_log, ssd_d, ssd_norm_w, w_out_odd, w_up, w_down):
    bsz, seq, d = x.shape
    assert bsz == 1 and d == D_MODEL
    n_ctx = ctx.shape[1]
    hcat = jnp.concatenate([ctx[0], x[0]], axis=0)
    m = hcat.shape[0]
    tm = _tile(m, 2112, 16)
    tm_acc = _tile(m, 1056, 16)
    mods = _modulation(c_ctx, c[0], w_mod, b_mod).reshape(DEPTH, 2, N_MOD, d)
    cos, sin = _rope_tables(n_ctx, seq)
    s5_ops = jax.vmap(_s5_operators)(s5_lam_re, s5_lam_im, s5_log_dt, s5_b_re, s5_b_im, s5_c_re, s5_c_im, s5_d)
    for i in range(DEPTH):
        md = mods[i]
        g = norm_g[i]
        j = i // 2
        if i % 2 == 0:
            lam_init = 0.8 - 0.6 * math.exp(-0.3 * i)
            mix = _even_mixer(hcat, g[0], md[:, 0], md[:, 1], w_in_even, j, diff_lam[j], diff_subln[j],
                              s5_ops, s5_w_glu[j], s5_b_glu[j], cos, sin, lam_init, n_ctx=n_ctx, tm=tm)
            w_out = w_out_even
        else:
            mix = _odd_mixer(hcat, g[0], md[:, 0], md[:, 1], w_in_odd, j, conv_w[j], conv_b[j], ssd_dt_bias[j],
                             ssd_a_log[j], ssd_d[j], ssd_norm_w[j], n_ctx=n_ctx, tm=tm)
            w_out = w_out_odd
        hcat = _matmul_resid(mix, w_out, j, hcat, g[1], md[:, 2], n_ctx=n_ctx, tm=tm_acc, tk=512, name="out_proj")
        hcat = _mlp_resid(hcat, g[2], md[:, 3], md[:, 4], w_up, w_down, i, g[3], md[:, 5],
                          n_ctx=n_ctx, tm=tm_acc, tn=512)
    return hcat[n_ctx:][None]
```

```python
import functools
import math

import jax
import jax.numpy as jnp
from jax import lax
from jax.experimental import pallas as pl
from jax.experimental.pallas import tpu as pltpu

F32 = jnp.float32
BF16 = jnp.bfloat16

D_MODEL = 2048
DEPTH = 4
GRID_W = 64
N_MOD = 6
RMS_EPS = 1e-6

DIFF_HEADS = 8
DIFF_QK_DIM = 64
DIFF_V_DIM = 128
DIFF_QK_W = 1024
DIFF_V_W = 1024
ROPE_BASE = 10000.0
ROPE_AXIS_DIM = 32
LOG2_E = math.log2(math.e)
VT_ROWS = DIFF_V_DIM + 16

S5_WIDTH = 1024
S5_P = 16
S5_GROUPS = 64
S5_STATE = 64
S5_T = 16
S5_ROW = S5_T * S5_P
S5_GB = 128 // S5_P
S5_SCAN_STEPS = 10

SSD_INNER = 4096
SSD_HEAD_DIM = 64
SSD_HEADS = 64
SSD_GROUPS = 8
SSD_HPG = 8
SSD_STATE = 128
SSD_CONV = 5
SSD_CHUNK = 128
SSD_CONV_CH = 6144

LANE = 128
VMEM_LIMIT = 56 * 1024 * 1024


def _tile(n, target, mult):
    best = None
    for cand in range(mult, min(n, target) + 1, mult):
        if n % cand == 0:
            best = cand
    assert best is not None, (n, target, mult)
    return best


def _cparams(sem):
    return pltpu.CompilerParams(dimension_semantics=sem, vmem_limit_bytes=VMEM_LIMIT)


def _rms_rows(x, g):
    return x * lax.rsqrt(jnp.mean(x * x, axis=-1, keepdims=True) + RMS_EPS) * g


def _pick_rows(sel_ctx, ref):
    return jnp.where(sel_ctx, ref[0:1, :], ref[1:2, :])


ROW_CHUNK = 64


def _for_row_chunks(n_rows, row0, ctx_rows, fn):
    rc = _tile(n_rows, ROW_CHUNK, 16)

    def body(r, carry):
        start = pl.multiple_of(r * rc, rc)
        is_ctx = (row0 + start + lax.broadcasted_iota(jnp.int32, (rc, 1), 0)) < ctx_rows
        fn(pl.ds(start, rc), is_ctx)
        return carry

    lax.fori_loop(0, n_rows // rc, body, 0)


def _normmod_store(h_ref, g_ref, shift_ref, scale_ref, a_scr, *, row0, ctx_rows):
    def fn(rows, is_ctx):
        y = _rms_rows(h_ref[rows, :], g_ref[...])
        a_scr[rows, :] = (y * (1.0 + _pick_rows(is_ctx, scale_ref)) + _pick_rows(is_ctx, shift_ref)).astype(BF16)

    _for_row_chunks(h_ref.shape[0], row0, ctx_rows, fn)


def _gated_resid_store(o_ref, h_ref, g_ref, gate_ref, *, row0, ctx_rows):
    def fn(rows, is_ctx):
        o_ref[rows, :] = h_ref[rows, :] + _pick_rows(is_ctx, gate_ref) * _rms_rows(o_ref[rows, :], g_ref[...])

    _for_row_chunks(o_ref.shape[0], row0, ctx_rows, fn)


def _mod_kernel(c_ref, w_ref, b_ref, o_ref):
    a = c_ref[...]
    a = a * jax.nn.sigmoid(a)
    for n0 in range(0, w_ref.shape[2], LANE):
        cols = slice(n0, n0 + LANE)
        w = w_ref[0, :, cols]
        rows = [jnp.sum(w * a[r], axis=0, keepdims=True) for r in range(2)]
        o_ref[0, :, cols] = jnp.concatenate(rows, axis=0) + b_ref[0, :, cols]


def _modulation(c_ctx, c, w_mod, b_mod):
    depth, d, n = w_mod.shape
    tn = _tile(n, 2048, LANE)
    c_lanes = jnp.broadcast_to(jnp.stack([c_ctx, c])[:, :, None], (2, d, LANE)).astype(F32)
    return pl.pallas_call(
        _mod_kernel,
        grid=(depth, n // tn),
        in_specs=[pl.BlockSpec((2, d, LANE), lambda l, j: (0, 0, 0)),
                  pl.BlockSpec((1, d, tn), lambda l, j: (l, 0, j)),
                  pl.BlockSpec((1, 1, tn), lambda l, j: (l, 0, j))],
        out_specs=pl.BlockSpec((1, 2, tn), lambda l, j: (l, 0, j)),
        out_shape=jax.ShapeDtypeStruct((depth, 2, n), F32),
        compiler_params=_cparams(("arbitrary", "arbitrary")),
        name="modulation",
    )(c_lanes, w_mod, b_mod.reshape(depth, 1, n))


def _normmod_matmul_kernel(h_ref, g_ref, shift_ref, scale_ref, w_ref, *rest, ctx_rows, row_step, tail):
    if tail:
        wt_ref, o_ref, ot_ref, a_scr = rest
    else:
        o_ref, a_scr = rest

    @pl.when(pl.program_id(1) == 0)
    def _():
        _normmod_store(h_ref, g_ref, shift_ref, scale_ref, a_scr,
                       row0=pl.program_id(0) * row_step, ctx_rows=ctx_rows)
        if tail:
            ot_ref[...] = jnp.dot(a_scr[...], wt_ref[...].astype(BF16), preferred_element_type=F32)

    o_ref[...] = jnp.dot(a_scr[...], w_ref[...].astype(BF16), preferred_element_type=F32)


ACC_COLS = 512


def _accumulate(o_ref, lhs, w_ref, k):
    @pl.when(k == 0)
    def _():
        o_ref[...] = jnp.zeros(o_ref.shape, F32)

    for n0 in range(0, o_ref.shape[1], ACC_COLS):
        cols = slice(n0, n0 + ACC_COLS)
        o_ref[:, cols] += jnp.dot(lhs, w_ref[:, cols].astype(BF16), preferred_element_type=F32)


def _normmod_matmul(h, g, shift, scale, w, layer, *, n_ctx, tm, tn, name, n_main=None):
    m, d = h.shape
    n = w.shape[2]
    n_main = n if n_main is None else n_main
    n_tail = n - n_main
    assert n_main % tn == 0 and (n_tail == 0 or n_main % n_tail == 0)
    in_specs = [pl.BlockSpec((tm, d), lambda i, j: (i, 0), pipeline_mode=pl.Buffered(1)),
                pl.BlockSpec((1, d), lambda i, j: (0, 0)),
                pl.BlockSpec((2, d), lambda i, j: (0, 0)),
                pl.BlockSpec((2, d), lambda i, j: (0, 0)),
                pl.BlockSpec((None, d, tn), lambda i, j: (layer, 0, j))]
    out_specs = [pl.BlockSpec((tm, tn), lambda i, j: (i, j))]
    out_shape = [jax.ShapeDtypeStruct((m, n_main), F32)]
    args = [h, g.reshape(1, d), shift, scale, w]
    if n_tail:
        in_specs.append(pl.BlockSpec((None, d, n_tail), lambda i, j: (layer, 0, n_main // n_tail)))
        out_specs.append(pl.BlockSpec((tm, n_tail), lambda i, j: (i, 0)))
        out_shape.append(jax.ShapeDtypeStruct((m, n_tail), F32))
        args.append(w)
    outs = pl.pallas_call(
        functools.partial(_normmod_matmul_kernel, ctx_rows=n_ctx, row_step=tm, tail=bool(n_tail)),
        grid=(m // tm, n_main // tn),
        in_specs=in_specs,
        out_specs=out_specs,
        out_shape=out_shape,
        scratch_shapes=[pltpu.VMEM((tm, d), BF16)],
        compiler_params=_cparams(("arbitrary", "arbitrary")),
        name=name,
    )(*args)
    return outs if n_tail else outs[0]


def _matmul_resid_kernel(*refs, n_ctx, tm, nk, k_per_part):
    *a_refs, w_ref, h_ref, g_ref, gate_ref, o_ref = refs
    k = pl.program_id(1)
    lhs = a_refs[0][...]
    for part in range(1, len(a_refs)):
        lhs = jnp.where(k >= part * k_per_part, a_refs[part][...], lhs)
    _accumulate(o_ref, lhs, w_ref, k)

    @pl.when(k == nk - 1)
    def _():
        _gated_resid_store(o_ref, h_ref, g_ref, gate_ref, row0=pl.program_id(0) * tm, ctx_rows=n_ctx)


def _matmul_resid(a_parts, w, layer, h, g, gate, *, n_ctx, tm, tk, name):
    m, kpart = a_parts[0].shape
    assert all(a.shape == (m, kpart) for a in a_parts) and kpart % tk == 0
    d = w.shape[2]
    k_per_part = kpart // tk
    nk = k_per_part * len(a_parts)

    def part_spec(part):
        return pl.BlockSpec((tm, tk), lambda i, k: (i, jnp.clip(k - part * k_per_part, 0, k_per_part - 1)))

    return pl.pallas_call(
        functools.partial(_matmul_resid_kernel, n_ctx=n_ctx, tm=tm, nk=nk, k_per_part=k_per_part),
        grid=(m // tm, nk),
        in_specs=[part_spec(part) for part in range(len(a_parts))]
        + [pl.BlockSpec((None, tk, d), lambda i, k: (layer, k, 0)),
           pl.BlockSpec((tm, d), lambda i, k: (i, 0)),
           pl.BlockSpec((1, d), lambda i, k: (0, 0)),
           pl.BlockSpec((2, d), lambda i, k: (0, 0))],
        out_specs=pl.BlockSpec((tm, d), lambda i, k: (i, 0)),
        out_shape=jax.ShapeDtypeStruct((m, d), F32),
        compiler_params=_cparams(("arbitrary", "arbitrary")),
        name=name,
    )(*a_parts, w, h, g.reshape(1, d), gate)


def _mlp_kernel(h_ref, g_in_ref, shift_ref, scale_ref, wup_ref, wdn_ref, g_out_ref, gate_ref, o_ref, a_scr,
                *, n_ctx, tm, nj):
    row0 = pl.program_id(0) * tm
    j = pl.program_id(1)

    @pl.when(j == 0)
    def _():
        _normmod_store(h_ref, g_in_ref, shift_ref, scale_ref, a_scr, row0=row0, ctx_rows=n_ctx)

    hid = jnp.dot(a_scr[...], wup_ref[...].astype(BF16), preferred_element_type=F32)
    hid = jnp.square(jnp.maximum(hid, 0.0)).astype(BF16)
    _accumulate(o_ref, hid, wdn_ref, j)

    @pl.when(j == nj - 1)
    def _():
        _gated_resid_store(o_ref, h_ref, g_out_ref, gate_ref, row0=row0, ctx_rows=n_ctx)


def _mlp_resid(h, g_in, shift, scale, w_up, w_down, layer, g_out, gate, *, n_ctx, tm, tn):
    m, d = h.shape
    hidden = w_up.shape[2]
    nj = hidden // tn
    row = lambda i, j: (i, 0)
    fixed = lambda i, j: (0, 0)
    return pl.pallas_call(
        functools.partial(_mlp_kernel, n_ctx=n_ctx, tm=tm, nj=nj),
        grid=(m // tm, nj),
        in_specs=[pl.BlockSpec((tm, d), row, pipeline_mode=pl.Buffered(1)),
                  pl.BlockSpec((1, d), fixed),
                  pl.BlockSpec((2, d), fixed),
                  pl.BlockSpec((2, d), fixed),
                  pl.BlockSpec((None, d, tn), lambda i, j: (layer, 0, j)),
                  pl.BlockSpec((None, tn, d), lambda i, j: (layer, j, 0)),
                  pl.BlockSpec((1, d), fixed),
                  pl.BlockSpec((2, d), fixed)],
        out_specs=pl.BlockSpec((tm, d), row, pipeline_mode=pl.Buffered(1)),
        out_shape=jax.ShapeDtypeStruct((m, d), F32),
        scratch_shapes=[pltpu.VMEM((tm, d), BF16)],
        compiler_params=_cparams(("arbitrary", "arbitrary")),
        name="mlp",
    )(h, g_in.reshape(1, d), shift, scale, w_up, w_down, g_out.reshape(1, d), gate)


def _rope_tables(n_ctx, seq_len):
    rows = seq_len // GRID_W
    row = jnp.repeat(jnp.arange(rows, dtype=F32), GRID_W)
    col = jnp.tile(jnp.arange(GRID_W, dtype=F32), rows)
    inv = ROPE_BASE ** (-jnp.arange(0, ROPE_AXIS_DIM, 2, dtype=F32) / ROPE_AXIS_DIM)
    ang_r = row[:, None] * inv
    ang_c = col[:, None] * inv
    ang = jnp.concatenate([ang_r, ang_r, ang_c, ang_c], axis=-1)
    cos = jnp.concatenate([jnp.ones((n_ctx, DIFF_QK_DIM), F32), jnp.cos(ang)], axis=0)
    sin = jnp.concatenate([jnp.zeros((n_ctx, DIFF_QK_DIM), F32), jnp.sin(ang)], axis=0)
    sign = jnp.where((jnp.arange(DIFF_QK_DIM) % 32) < 16, -1.0, 1.0).astype(F32)
    sin = sin * sign
    return jnp.tile(cos, (1, 2)), jnp.tile(sin, (1, 2))


def _rope_kernel(p_ref, cos_ref, sin_ref, q_ref, k_ref, vt_ref):
    cos = cos_ref[...]
    sin = sin_ref[...]
    low = (lax.broadcasted_iota(jnp.int32, (1, LANE), 1) % 32) < 16

    def rope(x):
        rot = jnp.where(low, pltpu.roll(x, LANE - 16, axis=1), pltpu.roll(x, 16, axis=1))
        return x * cos + rot * sin

    for c in range(DIFF_QK_W // LANE):
        sl = slice(c * LANE, (c + 1) * LANE)
        q_ref[:, sl] = (rope(p_ref[:, sl]) * (DIFF_QK_DIM ** -0.5 * LOG2_E)).astype(BF16)
        k_ref[:, sl] = rope(p_ref[:, DIFF_QK_W + c * LANE:DIFF_QK_W + (c + 1) * LANE]).astype(BF16)
    for h in range(DIFF_HEADS):
        v = p_ref[:, 2 * DIFF_QK_W + h * DIFF_V_DIM:2 * DIFF_QK_W + (h + 1) * DIFF_V_DIM]
        vt_ref[h, 0:DIFF_V_DIM, :] = v.T.astype(BF16)
        vt_ref[h, DIFF_V_DIM:VT_ROWS, :] = jnp.ones((VT_ROWS - DIFF_V_DIM, v.shape[0]), BF16)


def _rope_split(p, cos, sin, *, tm):
    m = p.shape[0]
    w = 2 * DIFF_QK_W + DIFF_V_W
    out = jax.ShapeDtypeStruct((m, DIFF_QK_W), BF16)
    return pl.pallas_call(
        _rope_kernel,
        grid=(m // tm,),
        in_specs=[pl.BlockSpec((tm, w), lambda i: (i, 0)),
                  pl.BlockSpec((tm, LANE), lambda i: (i, 0)),
                  pl.BlockSpec((tm, LANE), lambda i: (i, 0))],
        out_specs=[pl.BlockSpec((tm, DIFF_QK_W), lambda i: (i, 0)),
                   pl.BlockSpec((tm, DIFF_QK_W), lambda i: (i, 0)),
                   pl.BlockSpec((DIFF_HEADS, VT_ROWS, tm), lambda i: (0, 0, i))],
        out_shape=[out, out, jax.ShapeDtypeStruct((DIFF_HEADS, VT_ROWS, m), BF16)],
        compiler_params=_cparams(("arbitrary",)),
        name="rope_split",
    )(p, cos, sin)


def _attn_kernel(lam_ref, q0_ref, q1_ref, k0_ref, k1_ref, vt_ref, subln_ref, o_ref, acc_scr, s_scr,
                 *, tk, nk, out_scale):
    tq = q0_ref.shape[0]
    lam = lam_ref[0, 0]
    lane_head = lax.broadcasted_iota(jnp.int32, (1, LANE), 1) // DIFF_QK_DIM
    nt = (((1,), (1,)), ((), ()))
    k_refs = (k0_ref, k1_ref)

    def keys(kb):
        start = kb * tk
        return pl.ds(start if isinstance(kb, int) else pl.multiple_of(start, tk), tk)

    for sub in range(2):
        qs = [jnp.where(lane_head == sub, q_ref[...], jnp.zeros((), BF16)) for q_ref in (q0_ref, q1_ref)]
        acc_scr[...] = jnp.zeros(acc_scr.shape, F32)

        def scores(kb, slot):
            for mp in range(2):
                s_scr[slot, mp] = lax.dot_general(k_refs[mp][keys(kb), :], qs[mp], nt,
                                                  preferred_element_type=F32)

        def softmax_pv(kb, slot, ms):
            vt = vt_ref[sub, :, keys(kb)]
            new_ms = []
            for mp in range(2):
                s = s_scr[slot, mp]
                m_new = jnp.maximum(ms[mp], jnp.max(s, axis=0, keepdims=True))
                alpha = jnp.exp2(ms[mp] - m_new)
                p = jnp.exp2((s - m_new).astype(BF16))
                acc_scr[mp] = alpha * acc_scr[mp] + jnp.dot(vt, p, preferred_element_type=F32)
                new_ms.append(m_new)
            return tuple(new_ms)

        def pair(i, ms):
            scores(2 * i + 1, 1)
            ms = softmax_pv(2 * i, 0, ms)
            scores(2 * i + 2, 0)
            return softmax_pv(2 * i + 1, 1, ms)

        m_init = jnp.full((1, tq), -jnp.inf, F32)
        scores(0, 0)
        ms = lax.fori_loop(0, (nk - 1) // 2, pair, (m_init, m_init))
        if (nk - 1) % 2 == 1:
            scores(nk - 1, 1)
            ms = softmax_pv(nk - 2, 0, ms)
            softmax_pv(nk - 1, 1, ms)
        else:
            softmax_pv(nk - 1, 0, ms)
        o0 = acc_scr[0, 0:DIFF_V_DIM, :] / acc_scr[0, DIFF_V_DIM:DIFF_V_DIM + 1, :]
        o1 = acc_scr[1, 0:DIFF_V_DIM, :] / acc_scr[1, DIFF_V_DIM:DIFF_V_DIM + 1, :]
        o = o0 - lam * o1
        o = o * lax.rsqrt(jnp.mean(o * o, axis=0, keepdims=True) + RMS_EPS) * (subln_ref[...] * out_scale)
        o_ref[:, sub * DIFF_V_DIM:(sub + 1) * DIFF_V_DIM] = o.T.astype(o_ref.dtype)


def _diff_attention(lam, q, k, vt, subln, *, q_row0, n_q, n_k, tq, tk, lam_init, name):
    pairs = DIFF_HEADS // 2
    assert q_row0 % tq == 0 and n_q % tq == 0
    qb0 = q_row0 // tq
    nk = n_k // tk
    return pl.pallas_call(
        functools.partial(_attn_kernel, tk=tk, nk=nk, out_scale=1.0 - lam_init),
        grid=(pairs, n_q // tq),
        in_specs=[pl.BlockSpec(memory_space=pltpu.SMEM),
                  pl.BlockSpec((tq, LANE), lambda j, i: (qb0 + i, j)),
                  pl.BlockSpec((tq, LANE), lambda j, i: (qb0 + i, pairs + j)),
                  pl.BlockSpec((n_k, LANE), lambda j, i: (0, j)),
                  pl.BlockSpec((n_k, LANE), lambda j, i: (0, pairs + j)),
                  pl.BlockSpec((2, VT_ROWS, n_k), lambda j, i: (j, 0, 0)),
                  pl.BlockSpec((DIFF_V_DIM, 1), lambda j, i: (0, 0))],
        out_specs=pl.BlockSpec((tq, 2 * DIFF_V_DIM), lambda j, i: (i, j)),
        out_shape=jax.ShapeDtypeStruct((n_q, DIFF_V_W), BF16),
        scratch_shapes=[pltpu.VMEM((2, VT_ROWS, tq), F32), pltpu.VMEM((2, 2, tk, tq), F32)],
        compiler_params=_cparams(("arbitrary", "arbitrary")),
        name=name,
    )(lam, q, q, k, k, vt, subln.reshape(DIFF_V_DIM, 1).astype(F32))


def _s5_operators(lam_re, lam_im, log_dt, b_re, b_im, c_re, c_im, d_skip):
    t = S5_T
    steps = jnp.arange(t + 1, dtype=F32)
    ops = {}
    kerns = []
    for d in range(2):
        lam = lax.complex(lam_re[d].astype(F32), lam_im[d].astype(F32))
        dt = jnp.exp(log_dt[d].astype(F32))[:, None]
        lam_dt = lam * dt
        lam_bar = jnp.exp(lam_dt)
        b_bar = ((lam_bar - 1) / lam)[..., None] * lax.complex(b_re[d].astype(F32), b_im[d].astype(F32))
        c_mat = lax.complex(c_re[d].astype(F32), c_im[d].astype(F32))
        pw = jnp.exp(lam_dt[None] * steps[:, None, None])
        kerns.append(jnp.real(jnp.einsum('gpn,tgn,gnq->gtqp', c_mat, pw[:t], b_bar,
                                         precision=lax.Precision.HIGHEST)))
        pw_in = pw[:t][::-1] if d == 0 else pw[:t]
        vin = pw_in[:, :, :, None] * b_bar[None]
        vin = jnp.transpose(vin, (1, 0, 3, 2)).reshape(S5_GROUPS, S5_ROW, S5_STATE)
        v_d = jnp.concatenate([jnp.real(vin), jnp.imag(vin)], axis=-1)
        pw_out = pw[1:] if d == 0 else pw[1:][::-1]
        wout = c_mat[None] * pw_out[:, :, None, :]
        wout = jnp.transpose(wout, (1, 3, 0, 2)).reshape(S5_GROUPS, S5_STATE, S5_ROW)
        w_d = jnp.concatenate([jnp.real(wout), -jnp.imag(wout)], axis=1)
        kk = (2.0 ** jnp.arange(S5_SCAN_STEPS, dtype=F32)) * t
        ak = jnp.exp(lam_dt[:, None, :] * kk[None, :, None])
        a1 = jnp.concatenate([jnp.real(ak), jnp.real(ak)], axis=-1)
        a2 = jnp.concatenate([-jnp.imag(ak), jnp.imag(ak)], axis=-1)
        ops[d] = (v_d, w_d, a1, a2)
    kf, kb = kerns
    skip = jnp.eye(S5_P, dtype=F32)[None, None] * d_skip.astype(F32)[:, None, None, :]
    k_lag = jnp.concatenate([kb[:, :0:-1], kf[:, :1] + kb[:, :1] + skip, kf[:, 1:]], axis=1)
    rows = jnp.stack([k_lag[:, t - 1 - s:2 * t - 1 - s] for s in range(t)], axis=1)
    m_all = jnp.transpose(rows, (0, 1, 3, 2, 4)).reshape(S5_GROUPS, S5_ROW, S5_ROW)
    v_all = jnp.concatenate([ops[0][0], ops[1][0]], axis=-1)
    w_all = jnp.concatenate([ops[0][1], ops[1][1]], axis=1)
    pw_all = jnp.stack([ops[0][2], ops[0][3], ops[1][2], ops[1][3]], axis=1)
    return m_all.astype(BF16), v_all.astype(BF16), w_all.astype(BF16), pw_all


def _shift_rows(x, s, up):
    n = x.shape[0]
    if s >= n:
        return jnp.zeros_like(x)
    if s % 8 == 0:
        z = jnp.zeros((s, x.shape[1]), x.dtype)
        return jnp.concatenate([x[s:], z], axis=0) if up else jnp.concatenate([z, x[:n - s]], axis=0)
    row = lax.broadcasted_iota(jnp.int32, x.shape, 0)
    if up:
        return jnp.where(row < n - s, pltpu.roll(x, n - s, axis=0), 0.0)
    return jnp.where(row >= s, pltpu.roll(x, s, axis=0), 0.0)


def _s5_chunk_scan(z, a1_ref, a2_ref, up):
    x = _shift_rows(z, 1, up)
    n = x.shape[0]
    k = 0
    while (1 << k) < n:
        xs = _shift_rows(x, 1 << k, up)
        x = x + a1_ref[k:k + 1, :] * xs + a2_ref[k:k + 1, :] * pltpu.roll(xs, S5_STATE, axis=1)
        k += 1
    return x


def _s5_regroup_matrix():
    n = S5_T * LANE
    src = jnp.arange(n)
    t, g, q = src // LANE, (src % LANE) // S5_P, src % S5_P
    dst = g * S5_ROW + t * S5_P + q
    return (dst[:, None] == jnp.arange(n)[None, :]).astype(BF16)


def _s5_kernel(x_ref, perm_ref, m_ref, v_ref, w_ref, pw_ref, y_ref, *, n_ctx_chunks):
    nc = n_ctx_chunks
    rows = x_ref.shape[0] // S5_T
    perm = perm_ref[...]
    x = jnp.concatenate([x_ref[pl.ds(t, rows, stride=S5_T), :].astype(BF16) for t in range(S5_T)],
                        axis=1)
    xp = jnp.dot(x, perm, preferred_element_type=F32).astype(BF16)
    ys = []
    for g in range(S5_GB):
        u = xp[:, g * S5_ROW:(g + 1) * S5_ROW]
        y = jnp.dot(u, m_ref[g], preferred_element_type=F32)
        z = jnp.dot(u, v_ref[g], preferred_element_type=F32)
        zf = z[:, :LANE]
        zb = z[:, LANE:]
        sf = _s5_chunk_scan(zf, pw_ref.at[g, 0], pw_ref.at[g, 1], up=False)
        zb = jnp.concatenate([zb[nc:], zb[:nc]], axis=0)
        sb = _s5_chunk_scan(zb, pw_ref.at[g, 2], pw_ref.at[g, 3], up=True)
        nl = sb.shape[0] - nc
        sb = jnp.concatenate([sb[nl:], sb[:nl]], axis=0)
        s = jnp.concatenate([sf, sb], axis=1).astype(BF16)
        ys.append(y + jnp.dot(s, w_ref[g], preferred_element_type=F32))
    y = jnp.concatenate(ys, axis=1)
    yt = lax.dot_general(y.astype(BF16), perm, (((1,), (1,)), ((), ())),
                         preferred_element_type=F32)
    for t in range(S5_T):
        y_ref[pl.ds(t, rows, stride=S5_T), :] = yt[:, t * LANE:(t + 1) * LANE]


def _s5_scan(p, col0, operators, layer, *, n_ctx_chunks):
    m_all, v_all, w_all, pw_all = operators
    m = p.shape[0]
    assert m % S5_T == 0 and m // S5_T <= 2 ** S5_SCAN_STEPS
    cb0 = col0 // LANE
    spec = pl.BlockSpec((None, S5_GB, S5_ROW, S5_ROW), lambda i: (layer, i, 0, 0))
    n = S5_T * LANE
    return pl.pallas_call(
        functools.partial(_s5_kernel, n_ctx_chunks=n_ctx_chunks),
        grid=(S5_WIDTH // LANE,),
        in_specs=[pl.BlockSpec((m, LANE), lambda i: (0, cb0 + i)),
                  pl.BlockSpec((n, n), lambda i: (0, 0)), spec, spec, spec,
                  pl.BlockSpec((None, S5_GB, 4, S5_SCAN_STEPS, LANE), lambda i: (layer, i, 0, 0, 0))],
        out_specs=pl.BlockSpec((m, LANE), lambda i: (0, i)),
        out_shape=jax.ShapeDtypeStruct((m, S5_WIDTH), F32),
        compiler_params=_cparams(("arbitrary",)),
        name="s5_scan",
    )(p, _s5_regroup_matrix(), m_all, v_all, w_all, pw_all)


def _glu_kernel(y_ref, w_ref, b_ref, o_ref):
    g = jax.nn.gelu(y_ref[...])
    gate = jnp.dot(g.astype(BF16), w_ref[...], preferred_element_type=F32) + b_ref[...]
    o_ref[...] = (g * jax.nn.sigmoid(gate)).astype(o_ref.dtype)


def _s5_glu(y, w_glu, b_glu, *, tm):
    m, w = y.shape
    return pl.pallas_call(
        _glu_kernel,
        grid=(m // tm,),
        in_specs=[pl.BlockSpec((tm, w), lambda i: (i, 0)),
                  pl.BlockSpec((w, w), lambda i: (0, 0)),
                  pl.BlockSpec((1, w), lambda i: (0, 0))],
        out_specs=pl.BlockSpec((tm, w), lambda i: (i, 0)),
        out_shape=jax.ShapeDtypeStruct((m, w), BF16),
        compiler_params=_cparams(("arbitrary",)),
        name="s5_glu",
    )(y, w_glu.astype(BF16), b_glu.reshape(1, w).astype(F32))


CONV_HALO = 8


def _conv_kernel(x_ref, w_ref, b_ref, o_ref, pad_scr, *, n_ctx, tile):
    m = x_ref.shape[0]
    half = SSD_CONV // 2
    zeros = jnp.zeros((CONV_HALO, LANE), F32)
    segs = ((0, n_ctx, CONV_HALO), (n_ctx, m, 2 * CONV_HALO))
    pad_scr[0:CONV_HALO, :] = zeros
    pad_scr[CONV_HALO + n_ctx:2 * CONV_HALO + n_ctx, :] = zeros
    pad_scr[2 * CONV_HALO + m:3 * CONV_HALO + m, :] = zeros
    for lo, hi, off in segs:
        pad_scr[lo + off:hi + off, :] = x_ref[lo:hi, :]
    bias = b_ref[...]
    for lo, hi, off in segs:
        for r0 in range(lo, hi, tile):
            acc = bias
            for tap in range(SSD_CONV):
                acc = acc + w_ref[tap:tap + 1, :] * pad_scr[r0 + off + tap - half:r0 + off + tap - half + tile, :]
            o_ref[r0:r0 + tile, :] = acc * jax.nn.sigmoid(acc)


def _conv_silu(p, conv_w, conv_b, *, n_ctx, col0):
    m = p.shape[0]
    tile = math.gcd(n_ctx, m - n_ctx, 256)
    assert tile % 8 == 0
    cb0 = col0 // LANE
    return pl.pallas_call(
        functools.partial(_conv_kernel, n_ctx=n_ctx, tile=tile),
        grid=(SSD_CONV_CH // LANE,),
        in_specs=[pl.BlockSpec((m, LANE), lambda j: (0, cb0 + j)),
                  pl.BlockSpec((SSD_CONV, LANE), lambda j: (0, j)),
                  pl.BlockSpec((1, LANE), lambda j: (0, j))],
        out_specs=pl.BlockSpec((m, LANE), lambda j: (0, j)),
        out_shape=jax.ShapeDtypeStruct((m, SSD_CONV_CH), F32),
        scratch_shapes=[pltpu.VMEM((m + 3 * CONV_HALO, LANE), F32)],
        compiler_params=_cparams(("arbitrary",)),
        name="conv_silu",
    )(p, conv_w, conv_b.reshape(1, SSD_CONV_CH))


def _split_dot(x, e2):
    hi = x.astype(BF16)
    lo = (x - hi.astype(F32)).astype(BF16)
    return jnp.dot(jnp.concatenate([hi, lo], axis=1), e2, preferred_element_type=F32)


def _split3(x):
    x1 = x.astype(BF16)
    r1 = x - x1.astype(F32)
    x2 = r1.astype(BF16)
    x3 = (r1 - x2.astype(F32)).astype(BF16)
    return x1, x2, x3


def _ssd_kernel(x_ref, b_ref, c_ref, dt_ref, bias_ref, a_ref, tri_ref, e_ref, y_ref, state_scr):
    t = x_ref.shape[0]
    backward = pl.program_id(0) == 1

    @pl.when(pl.program_id(1) == 0)
    def _():
        state_scr[...] = jnp.zeros(state_scr.shape, F32)

    tri = tri_ref[0]
    expand = e_ref[...]
    dt = jax.nn.softplus(dt_ref[...] + bias_ref[...])
    a = dt * a_ref[...]
    dt = jnp.where(backward, pltpu.roll(dt, SSD_HEADS, axis=1), dt)
    a = jnp.where(backward, pltpu.roll(a, SSD_HEADS, axis=1), a)
    at = a.T
    cs = sum(jnp.dot(tri, ai, preferred_element_type=F32) for ai in _split3(a))
    cst = sum(lax.dot_general(ai, tri, (((1,), (1,)), ((), ())), preferred_element_type=F32)
              for ai in _split3(at))
    total = jnp.sum(a, axis=0, keepdims=True)
    dt_x = _split_dot(dt, expand)
    din_x = _split_dot(jnp.exp(total - cs), expand)
    dout_x = _split_dot(jnp.exp(cs), expand)
    cdec_x = _split_dot(jnp.broadcast_to(jnp.exp(total), (8, LANE)), expand)[0:1]

    xdt = x_ref[...] * dt_x
    xin = (xdt * din_x).astype(BF16)
    xdt = xdt.astype(BF16)
    lane_head = lax.broadcasted_iota(jnp.int32, (1, LANE), 1) // SSD_HEAD_DIM
    gw = SSD_HPG * SSD_HEAD_DIM
    for g in range(SSD_GROUPS):
        bf = b_ref[:, g * SSD_STATE:(g + 1) * SSD_STATE]
        bg = bf.astype(BF16)
        bgt = bf.T.astype(BF16)
        cg = c_ref[:, g * SSD_STATE:(g + 1) * SSD_STATE].astype(BF16)
        cb = lax.dot_general(cg, bg, (((1,), (1,)), ((), ())), preferred_element_type=F32)
        gs = slice(g * gw, (g + 1) * gw)
        st = state_scr[:, gs]
        y_off = jnp.dot(cg, st.astype(BF16), preferred_element_type=F32) * dout_x[:, gs]
        state_scr[:, gs] = st * cdec_x[:, gs] + jnp.dot(bgt, xin[:, gs], preferred_element_type=F32)
        for pr in range(SSD_HPG // 2):
            cols = slice(g * gw + pr * LANE, g * gw + (pr + 1) * LANE)
            xp = xdt[:, cols]
            mms, xhs = [], []
            for sub in range(2):
                h = g * SSD_HPG + pr * 2 + sub
                seg = jnp.exp(cs[:, h:h + 1] - cst[h:h + 1, :])
                mms.append(jnp.where(tri > 0, cb * seg, 0.0).astype(BF16))
                xhs.append(jnp.where(lane_head == sub, xp, jnp.zeros((), BF16)))
            yd = jnp.dot(jnp.concatenate(mms, axis=1), jnp.concatenate(xhs, axis=0),
                         preferred_element_type=F32)
            y_ref[0, :, cols] = yd + y_off[:, pr * LANE:(pr + 1) * LANE]


def _ssd_scan(xbc, dt_raw, bias, a_neg, *, n_ctx):
    m = xbc.shape[0]
    t = SSD_CHUNK
    nchunks = m // t
    ctx_chunks = n_ctx // t

    def pos(d, c):
        back = jnp.where(c < ctx_chunks, ctx_chunks - 1 - c, nchunks - 1 + ctx_chunks - c)
        return jnp.where(d == 0, c, back)

    idx = jnp.arange(t)
    tri = jnp.stack([idx[:, None] >= idx[None, :], idx[:, None] <= idx[None, :]]).astype(BF16)
    expand = (jnp.arange(LANE)[:, None] == (jnp.arange(SSD_INNER)[None, :] // SSD_HEAD_DIM)).astype(BF16)
    expand = jnp.concatenate([expand, expand], axis=0)
    xb = SSD_INNER // 1024
    return pl.pallas_call(
        _ssd_kernel,
        grid=(2, nchunks),
        in_specs=[pl.BlockSpec((t, SSD_INNER), lambda d, c: (pos(d, c), 0)),
                  pl.BlockSpec((t, 1024), lambda d, c: (pos(d, c), xb)),
                  pl.BlockSpec((t, 1024), lambda d, c: (pos(d, c), xb + 1)),
                  pl.BlockSpec((t, LANE), lambda d, c: (pos(d, c), 0)),
                  pl.BlockSpec((1, LANE), lambda d, c: (0, 0)),
                  pl.BlockSpec((1, LANE), lambda d, c: (0, 0)),
                  pl.BlockSpec((1, t, t), lambda d, c: (d, 0, 0)),
                  pl.BlockSpec((2 * LANE, SSD_INNER), lambda d, c: (0, 0))],
        out_specs=pl.BlockSpec((1, t, SSD_INNER), lambda d, c: (d, pos(d, c), 0)),
        out_shape=jax.ShapeDtypeStruct((2, m, SSD_INNER), F32),
        scratch_shapes=[pltpu.VMEM((SSD_STATE, SSD_INNER), F32)],
        compiler_params=_cparams(("arbitrary", "arbitrary")),
        name="ssd_scan",
    )(xbc, xbc, xbc, dt_raw, bias.reshape(1, LANE), a_neg.reshape(1, LANE), tri, expand)


def _ssd_finish_kernel(y_ref, x_ref, z_ref, d_ref, nw_ref, o_ref):
    z = z_ref[...]
    y = (x_ref[...] * d_ref[...] + y_ref[0] + y_ref[1]) * (z * jax.nn.sigmoid(z))
    o_ref[...] = _rms_rows(y, nw_ref[...]).astype(o_ref.dtype)


def _ssd_finish(y2, xbc, p, d_x, norm_w, *, tm):
    m = xbc.shape[0]
    w = SSD_INNER
    return pl.pallas_call(
        _ssd_finish_kernel,
        grid=(m // tm,),
        in_specs=[pl.BlockSpec((2, tm, w), lambda i: (0, i, 0)),
                  pl.BlockSpec((tm, w), lambda i: (i, 0)),
                  pl.BlockSpec((tm, w), lambda i: (i, 0)),
                  pl.BlockSpec((1, w), lambda i: (0, 0)),
                  pl.BlockSpec((1, w), lambda i: (0, 0))],
        out_specs=pl.BlockSpec((tm, w), lambda i: (i, 0)),
        out_shape=jax.ShapeDtypeStruct((m, w), BF16),
        compiler_params=_cparams(("arbitrary",)),
        name="ssd_finish",
    )(y2, xbc, p, d_x, norm_w.reshape(1, w))


def _even_mixer(hcat, g0, shift, scale, w_in, j, lam_p, subln, s5_ops, w_glu, b_glu, cos, sin, lam_init,
                *, n_ctx, tm):
    m = hcat.shape[0]
    p = _normmod_matmul(hcat, g0, shift, scale, w_in, j, n_ctx=n_ctx, tm=tm, tn=512,
                        name="in_proj_even")
    q, k, vt = _rope_split(p, cos, sin, tm=_tile(m, 256, LANE))
    lp = lam_p.astype(F32)
    lam = (jnp.exp(jnp.sum(lp[0] * lp[1])) - jnp.exp(jnp.sum(lp[2] * lp[3])) + lam_init).reshape(1, 1)
    tq = _tile(math.gcd(n_ctx, m - n_ctx), 256, LANE)
    o_lat = _diff_attention(lam, q, k, vt, subln, q_row0=n_ctx, n_q=m - n_ctx, n_k=m, tq=tq,
                            tk=_tile(m, 768, LANE), lam_init=lam_init, name="diff_attn")
    o_ctx = _diff_attention(lam, q, k, vt, subln, q_row0=0, n_q=n_ctx, n_k=n_ctx, tq=tq,
                            tk=_tile(n_ctx, 768, LANE), lam_init=lam_init, name="diff_attn_ctx")
    y = _s5_scan(p, 2 * DIFF_QK_W + DIFF_V_W, s5_ops, j, n_ctx_chunks=n_ctx // S5_T)
    s = _s5_glu(y, w_glu, b_glu, tm=_tile(m, 1056, 16))
    return [jnp.concatenate([o_ctx, o_lat], axis=0), s]


def _odd_mixer(hcat, g0, shift, scale, w_in, j, conv_w, conv_b, dt_bias, a_log, d_skip, norm_w, *, n_ctx, tm):
    p, dt_raw = _normmod_matmul(hcat, g0, shift, scale, w_in, j, n_ctx=n_ctx, tm=tm, tn=512,
                                name="in_proj_odd", n_main=SSD_INNER + SSD_CONV_CH)
    xbc = _conv_silu(p, conv_w, conv_b, n_ctx=n_ctx, col0=SSD_INNER)
    a_neg = -jnp.exp(a_log.astype(F32))
    y2 = _ssd_scan(xbc, dt_raw, dt_bias.astype(F32), a_neg, n_ctx=n_ctx)
    d_x = jnp.repeat(d_skip.astype(F32), SSD_HEAD_DIM).reshape(1, SSD_INNER)
    return [_ssd_finish(y2, xbc, p, d_x, norm_w, tm=_tile(xbc.shape[0], 176, 16))]


def kernel(x, c, ctx, c_ctx, w_mod, b_mod, norm_g, w_in_even, w_out_even, diff_lam, diff_subln, s5_lam_re, s5_lam_im, s5_log_dt, s5_b_re, s5_b_im, s5_c_re, s5_c_im, s5_d, s5_w_glu, s5_b_glu, w_in_odd, conv_w, conv_b, ssd_dt_bias, ssd_a_log, ssd_d, ssd_norm_w, w_out_odd, w_up, w_down):
    bsz, seq, d = x.shape
    assert bsz == 1 and d == D_MODEL
    n_ctx = ctx.shape[1]
    hcat = jnp.concatenate([ctx[0], x[0]], axis=0)
    m = hcat.shape[0]
    tm = _tile(m, 2112, 16)
    tm_acc = _tile(m, 1056, 16)
    mods = _modulation(c_ctx, c[0], w_mod, b_mod).reshape(DEPTH, 2, N_MOD, d)
    cos, sin = _rope_tables(n_ctx, seq)
    s5_ops = jax.vmap(_s5_operators)(s5_lam_re, s5_lam_im, s5_log_dt, s5_b_re, s5_b_im, s5_c_re, s5_c_im, s5_d)
    for i in range(DEPTH):
        md = mods[i]
        g = norm_g[i]
        j = i // 2
        if i % 2 == 0:
            lam_init = 0.8 - 0.6 * math.exp(-0.3 * i)
            mix = _even_mixer(hcat, g[0], md[:, 0], md[:, 1], w_in_even, j, diff_lam[j], diff_subln[j],
                              s5_ops, s5_w_glu[j], s5_b_glu[j], cos, sin, lam_init, n_ctx=n_ctx, tm=tm)
            w_out = w_out_even
        else:
            mix = _odd_mixer(hcat, g[0], md[:, 0], md[:, 1], w_in_odd, j, conv_w[j], conv_b[j], ssd_dt_bias[j],
                             ssd_a_log[j], ssd_d[j], ssd_norm_w[j], n_ctx=n_ctx, tm=tm)
            w_out = w_out_odd
        hcat = _matmul_resid(mix, w_out, j, hcat, g[1], md[:, 2], n_ctx=n_ctx, tm=tm_acc, tk=512, name="out_proj")
        hcat = _mlp_resid(hcat, g[2], md[:, 3], md[:, 4], w_up, w_down, i, g[3], md[:, 5],
                          n_ctx=n_ctx, tm=tm_acc, tn=512)
    return hcat[n_ctx:][None]
```
